```python
import jax, jax.numpy as jnp
from jax import lax
import numpy as np

D_MODEL = 2048
BATCH = 4
SEQ = 4096
DEPTH = 1

MIX_WIDTH = 2 * D_MODEL
ATT_HEADS = 8
ATT_HEAD_DIM = 128
ATT_WIDTH = ATT_HEADS * ATT_HEAD_DIM
ATT_BLOCK = 128
SSM_WIDTH = MIX_WIDTH - ATT_WIDTH
SSM_HEAD_DIM = 64
SSM_HEADS = SSM_WIDTH // SSM_HEAD_DIM
SSM_GROUPS = 8
SSM_STATE = 128
SSM_CHUNK = 128
CONV_WIDTH = 4
CONV_CH = SSM_WIDTH + 2 * SSM_GROUPS * SSM_STATE
IN_COLS = 3 * ATT_WIDTH + ATT_HEADS + SSM_WIDTH + CONV_CH + SSM_HEADS
N_EXPERTS = 32
TOP_K = 4
EXPERT_FF = D_MODEL
SWIGLU_LIMIT = 7.0
SWIGLU_ALPHA = 1.702
MOE_BLOCK = 256
PLE_DIM = 256
EPS = 1e-6

kernel_name = "hymba_fox_ssd_moe_sandwich_ple"


def rmsnorm(x, w):
    xf = x.astype(jnp.float32)
    y = xf * lax.rsqrt(jnp.mean(xf * xf, axis=-1, keepdims=True) + EPS)
    return (y * w.astype(jnp.float32)).astype(x.dtype)


def forgetting_attention(q, k, v, fg_logit):
    bsz, s, h, dh = q.shape
    q = jnp.swapaxes(q, 1, 2)
    k = jnp.swapaxes(k, 1, 2)
    v = jnp.swapaxes(v, 1, 2)
    c = jnp.cumsum(jax.nn.log_sigmoid(fg_logit.astype(jnp.float32)), axis=1)
    c = jnp.swapaxes(c, 1, 2)
    scale = dh ** -0.5
    outs = []
    for i0 in range(0, s, ATT_BLOCK):
        i1 = i0 + ATT_BLOCK
        logits = jnp.einsum('bhqd,bhkd->bhqk', q[:, :, i0:i1], k[:, :, :i1]).astype(jnp.float32) * scale
        logits = logits + c[:, :, i0:i1, None] - c[:, :, None, :i1]
        qpos = jnp.arange(i0, i1)[:, None]
        kpos = jnp.arange(i1)[None, :]
        logits = jnp.where(kpos <= qpos, logits, -jnp.inf)
        probs = jax.nn.softmax(logits, axis=-1).astype(v.dtype)
        outs.append(jnp.einsum('bhqk,bhkd->bhqd', probs, v[:, :, :i1]))
    o = jnp.concatenate(outs, axis=2)
    return jnp.swapaxes(o, 1, 2).reshape(bsz, s, h * dh)


def causal_depthwise_conv(u, w, bias):
    ch = u.shape[-1]
    y = lax.conv_general_dilated(u, w[:, None, :].astype(u.dtype), window_strides=(1,),
                                 padding=[(CONV_WIDTH - 1, 0)],
                                 dimension_numbers=('NWC', 'WIO', 'NWC'),
                                 feature_group_count=ch)
    return y + bias.astype(u.dtype)


def ssd_scan(x, dt, a, b_in, c_in):
    f32 = jnp.float32
    bsz, s, h, pdim = x.shape
    g, n = b_in.shape[2], b_in.shape[3]
    r = h // g
    nc, L = s // SSM_CHUNK, SSM_CHUNK
    xc = (x.astype(f32) * dt[..., None]).reshape(bsz, nc, L, g, r, pdim)
    adt = (dt * a).reshape(bsz, nc, L, g, r)
    bc = b_in.astype(f32).reshape(bsz, nc, L, g, n)
    cc = c_in.astype(f32).reshape(bsz, nc, L, g, n)
    a_cs = jnp.cumsum(adt, axis=2)
    seg = a_cs[:, :, :, None] - a_cs[:, :, None, :]
    tri = jnp.tril(jnp.ones((L, L), dtype=bool))[:, :, None, None]
    decay = jnp.exp(jnp.where(tri, seg, -jnp.inf))
    cb = jnp.einsum('bclgn,bcsgn->bclsg', cc, bc)
    y_diag = jnp.einsum('bclsg,bclsgr,bcsgrp->bclgrp', cb, decay, xc)
    decay_states = jnp.exp(a_cs[:, :, -1:] - a_cs)
    states = jnp.einsum('bclgn,bclgr,bclgrp->bcgrpn', bc, decay_states, xc)
    chunk_decay = jnp.exp(a_cs[:, :, -1])

    def step(carry, inp):
        st, dec = inp
        return carry * dec[..., None, None] + st, carry

    init = jnp.zeros((bsz, g, r, pdim, n), f32)
    _, prev = lax.scan(step, init, (jnp.moveaxis(states, 1, 0), jnp.moveaxis(chunk_decay, 1, 0)))
    prev = jnp.moveaxis(prev, 0, 1)
    y_off = jnp.einsum('bclgn,bcgrpn,bclgr->bclgrp', cc, prev, jnp.exp(a_cs))
    return (y_diag + y_off).reshape(bsz, s, h, pdim)


def hybrid_mixer(xn, w_in, b_fgate, conv_w, conv_b, dt_bias, a_log, d_skip,
                 attn_out_norm, ssm_out_norm, w_out):
    f32 = jnp.float32
    bsz, s, _ = xn.shape
    proj = xn @ w_in
    splits = np.cumsum([ATT_WIDTH, ATT_WIDTH, ATT_WIDTH, ATT_HEADS, SSM_WIDTH, CONV_CH]).tolist()
    q, k, v, fg, z, xbc, dt_raw = jnp.split(proj, splits, axis=-1)
    hs = (bsz, s, ATT_HEADS, ATT_HEAD_DIM)
    att = forgetting_attention(q.reshape(hs), k.reshape(hs), v.reshape(hs), fg + b_fgate.astype(fg.dtype))
    att = rmsnorm(att, attn_out_norm)
    xbc = jax.nn.silu(causal_depthwise_conv(xbc, conv_w, conv_b))
    xs, bm, cm = jnp.split(xbc, [SSM_WIDTH, SSM_WIDTH + SSM_GROUPS * SSM_STATE], axis=-1)
    dt = jax.nn.softplus(dt_raw.astype(f32) + dt_bias.astype(f32))
    a = -jnp.exp(a_log.astype(f32))
    xh = xs.reshape(bsz, s, SSM_HEADS, SSM_HEAD_DIM)
    y = ssd_scan(xh, dt, a, bm.reshape(bsz, s, SSM_GROUPS, SSM_STATE),
                 cm.reshape(bsz, s, SSM_GROUPS, SSM_STATE))
    y = y + d_skip.astype(f32)[:, None] * xh.astype(f32)
    y = y.reshape(bsz, s, SSM_WIDTH) * jax.nn.silu(z.astype(f32))
    yg = y.reshape(bsz, s, SSM_GROUPS, SSM_WIDTH // SSM_GROUPS)
    yg = yg * lax.rsqrt(jnp.mean(yg * yg, axis=-1, keepdims=True) + EPS)
    y = (yg.reshape(bsz, s, SSM_WIDTH) * ssm_out_norm.astype(f32)).astype(xn.dtype)
    return jnp.concatenate([att, y], axis=-1) @ w_out


def clamped_swiglu(gate, up):
    gate = jnp.minimum(gate, SWIGLU_LIMIT)
    up = jnp.clip(up, -SWIGLU_LIMIT, SWIGLU_LIMIT)
    return (up + 1.0) * (gate * jax.nn.sigmoid(gate * SWIGLU_ALPHA))


def moe_ffn(hn, w_router, b_router, w_gate, b_gate, w_up, b_up, w_down, b_down):
    bsz, s, d = hn.shape
    t = bsz * s
    xt = hn.reshape(t, d)
    logits = (xt @ w_router + b_router).astype(jnp.float32)
    top_logit, top_e = lax.top_k(logits, TOP_K)
    top_w = jax.nn.softmax(top_logit, axis=-1)
    n = t * TOP_K
    e_flat = top_e.reshape(n)
    order = jnp.argsort(e_flat, stable=True)
    e_sorted = e_flat[order]
    tok_sorted = order // TOP_K
    counts = jnp.bincount(e_flat, length=N_EXPERTS)
    group_start = jnp.cumsum(counts) - counts
    padded = ((counts + MOE_BLOCK - 1) // MOE_BLOCK) * MOE_BLOCK
    padded_end = jnp.cumsum(padded)
    padded_start = padded_end - padded
    dest = padded_start[e_sorted] + (jnp.arange(n) - group_start[e_sorted])
    n_blocks = -(-n // MOE_BLOCK) + N_EXPERTS
    buf = jnp.zeros((n_blocks * MOE_BLOCK, d), hn.dtype).at[dest].set(xt[tok_sorted])
    block_e = jnp.minimum(jnp.searchsorted(padded_end, jnp.arange(n_blocks) * MOE_BLOCK, side='right'),
                          N_EXPERTS - 1)

    def expert_block(args):
        xb, e = args
        act = clamped_swiglu(xb @ w_gate[e] + b_gate[e], xb @ w_up[e] + b_up[e])
        return act @ w_down[e] + b_down[e]

    yb = lax.map(expert_block, (buf.reshape(n_blocks, MOE_BLOCK, d), block_e))
    y_rows = yb.reshape(n_blocks * MOE_BLOCK, d)[dest]
    w_sorted = top_w.reshape(n)[order]
    out = jnp.zeros((t, d), jnp.float32).at[tok_sorted].add(y_rows.astype(jnp.float32) * w_sorted[:, None])
    return out.astype(hn.dtype).reshape(bsz, s, d)


def setup_inputs(seed: int = 0) -> dict:
    key = jax.random.key(seed)
    ks = iter(jax.random.split(key, 40))
    f32 = jnp.float32

    def nrm(shape, scale):
        return jax.random.normal(next(ks), shape, f32) * scale

    def gain(shape):
        return 1.0 + 0.05 * jax.random.normal(next(ks), shape, f32)

    dt0 = jnp.exp(jax.random.uniform(next(ks), (DEPTH, SSM_HEADS), f32, np.log(1e-3), np.log(1e-1)))
    inp = {}
    inp["x"] = nrm((BATCH, SEQ, D_MODEL), 1.0)
    inp["p"] = nrm((DEPTH, BATCH, SEQ, PLE_DIM), 1.0)
    inp["pre_mix_norm"] = gain((DEPTH, D_MODEL))
    inp["w_in"] = nrm((DEPTH, D_MODEL, IN_COLS), D_MODEL ** -0.5)
    inp["b_fgate"] = jax.random.uniform(next(ks), (DEPTH, ATT_HEADS), f32, 1.0, 5.0)
    inp["conv_w"] = nrm((DEPTH, CONV_WIDTH, CONV_CH), CONV_WIDTH ** -0.5)
    inp["conv_b"] = nrm((DEPTH, CONV_CH), 0.01)
    inp["dt_bias"] = dt0 + jnp.log(-jnp.expm1(-dt0))
    inp["a_log"] = jnp.log(jax.random.uniform(next(ks), (DEPTH, SSM_HEADS), f32, 1.0, 16.0))
    inp["d_skip"] = gain((DEPTH, SSM_HEADS))
    inp["attn_out_norm"] = gain((DEPTH, ATT_WIDTH))
    inp["ssm_out_norm"] = gain((DEPTH, SSM_WIDTH))
    inp["w_out"] = nrm((DEPTH, MIX_WIDTH, D_MODEL), MIX_WIDTH ** -0.5)
    inp["post_mix_norm"] = gain((DEPTH, D_MODEL))
    inp["pre_ffn_norm"] = gain((DEPTH, D_MODEL))
    inp["w_router"] = nrm((DEPTH, D_MODEL, N_EXPERTS), D_MODEL ** -0.5)
    inp["b_router"] = nrm((DEPTH, N_EXPERTS), 0.01)
    inp["w_gate"] = nrm((DEPTH, N_EXPERTS, D_MODEL, EXPERT_FF), D_MODEL ** -0.5)
    inp["b_gate"] = nrm((DEPTH, N_EXPERTS, EXPERT_FF), 0.01)
    inp["w_up"] = nrm((DEPTH, N_EXPERTS, D_MODEL, EXPERT_FF), D_MODEL ** -0.5)
    inp["b_up"] = nrm((DEPTH, N_EXPERTS, EXPERT_FF), 0.01)
    inp["w_down"] = nrm((DEPTH, N_EXPERTS, EXPERT_FF, D_MODEL), EXPERT_FF ** -0.5)
    inp["b_down"] = nrm((DEPTH, N_EXPERTS, D_MODEL), 0.01)
    inp["post_ffn_norm"] = gain((DEPTH, D_MODEL))
    inp["ple_norm"] = gain((DEPTH, D_MODEL))
    inp["w_ple_gate"] = nrm((DEPTH, D_MODEL, D_MODEL), D_MODEL ** -0.5)
    inp["w_ple_proj"] = nrm((DEPTH, PLE_DIM, D_MODEL), PLE_DIM ** -0.5)
    inp["ple_post_norm"] = gain((DEPTH, D_MODEL))
    return inp


def reference(x, p, pre_mix_norm, w_in, b_fgate, conv_w, conv_b, dt_bias, a_log, d_skip,
              attn_out_norm, ssm_out_norm, w_out, post_mix_norm, pre_ffn_norm, w_router, b_router,
              w_gate, b_gate, w_up, b_up, w_down, b_down, post_ffn_norm, ple_norm, w_ple_gate,
              w_ple_proj, ple_post_norm):
    h = x
    for i in range(DEPTH):
        xn = rmsnorm(h, pre_mix_norm[i])
        mixed = hybrid_mixer(xn, w_in[i], b_fgate[i], conv_w[i], conv_b[i], dt_bias[i], a_log[i],
                             d_skip[i], attn_out_norm[i], ssm_out_norm[i], w_out[i])
        h = h + rmsnorm(mixed, post_mix_norm[i])
        hn = rmsnorm(h, pre_ffn_norm[i])
        ff = moe_ffn(hn, w_router[i], b_router[i], w_gate[i], b_gate[i], w_up[i], b_up[i],
                     w_down[i], b_down[i])
        h = h + rmsnorm(ff, post_ffn_norm[i])
        gate = jax.nn.sigmoid(rmsnorm(h, ple_norm[i]) @ w_ple_gate[i])
        h = h + rmsnorm(gate * (p[i] @ w_ple_proj[i]), ple_post_norm[i])
    return h
```

```python
import functools

import jax
import jax.numpy as jnp
from jax import lax
from jax.experimental import pallas as pl
from jax.experimental.pallas import tpu as pltpu

F32 = jnp.float32
BF16 = jnp.bfloat16

ATT_HEADS = 8
ATT_HEAD_DIM = 128
SSM_HEAD_DIM = 64
SSM_GROUPS = 8
SSM_STATE = 128
SSM_CHUNK = 128
CONV_WIDTH = 4
N_EXPERTS = 32
TOP_K = 4
SWIGLU_LIMIT = 7.0
SWIGLU_ALPHA = 1.702
EPS = 1e-6

LANES = 128
SUBLANES = 8
VMEM_LIMIT_BYTES = 56 * 1024 * 1024

MOE_ROWS = 512


def _params(*semantics):
    return pltpu.CompilerParams(dimension_semantics=semantics, vmem_limit_bytes=VMEM_LIMIT_BYTES)


def _rms(x, gain):
    return x * lax.rsqrt(jnp.mean(x * x, axis=-1, keepdims=True) + EPS) * gain


def _sigmoid(x):
    return 1.0 / (1.0 + jnp.exp(-x))


def _softplus(x):
    return jnp.maximum(x, 0.0) + jnp.log1p(jnp.exp(-jnp.abs(x)))


def _split3(x):
    x1 = x.astype(BF16)
    r1 = x - x1.astype(F32)
    x2 = r1.astype(BF16)
    x3 = (r1 - x2.astype(F32)).astype(BF16)
    return x1, x2, x3


def _dot(a, b):
    return jnp.dot(a, b, preferred_element_type=F32)


def _dot_exact_rhs(x, m):
    x1, x2, x3 = _split3(x)
    return _dot(x1, m) + _dot(x2, m) + _dot(x3, m)


def _dot_exact_lhs(m, x):
    x1, x2, x3 = _split3(x)
    return _dot(m, x1) + _dot(m, x2) + _dot(m, x3)


def _rms_matmul_kernel(x_ref, g_ref, w_ref, o_ref, xn_ref):
    @pl.when(pl.program_id(1) == 0)
    def _():
        xn_ref[...] = _rms(x_ref[...], g_ref[...]).astype(xn_ref.dtype)

    if xn_ref.dtype == F32:
        acc = jnp.dot(xn_ref[...], w_ref[...], preferred_element_type=F32,
                      precision=lax.Precision.HIGHEST)
    else:
        acc = _dot(xn_ref[...], w_ref[...])
    o_ref[...] = acc.astype(o_ref.dtype)


def _rms_matmul(x, gain, w, out_dtype, tm, tn):
    m, k = x.shape
    n = w.shape[1]
    return pl.pallas_call(
        _rms_matmul_kernel,
        grid=(m // tm, n // tn),
        in_specs=[
            pl.BlockSpec((tm, k), lambda i, j: (i, 0)),
            pl.BlockSpec((1, k), lambda i, j: (0, 0)),
            pl.BlockSpec((k, tn), lambda i, j: (0, j)),
        ],
        out_specs=pl.BlockSpec((tm, tn), lambda i, j: (i, j)),
        out_shape=jax.ShapeDtypeStruct((m, n), out_dtype),
        scratch_shapes=[pltpu.VMEM((tm, k), w.dtype)],
        compiler_params=_params("parallel", "arbitrary"),
    )(x, gain.reshape(1, k), w)


def _fg_cumsum_kernel(fg_ref, b_ref, c_ref):
    seq = fg_ref.shape[2]
    row = lax.broadcasted_iota(jnp.int32, (LANES, LANES), 0)
    col = lax.broadcasted_iota(jnp.int32, (LANES, LANES), 1)
    triu = (row <= col).astype(BF16)
    carry = jnp.zeros((fg_ref.shape[1], 1), F32)
    for ci in range(seq // LANES):
        z = fg_ref[0, :, ci * LANES:(ci + 1) * LANES] + b_ref[...]
        ls = jnp.minimum(z, 0.0) - jnp.log1p(jnp.exp(-jnp.abs(z)))
        cs = _dot_exact_rhs(ls, triu) + carry
        c_ref[0, :, ci * LANES:(ci + 1) * LANES] = cs
        carry = cs[:, LANES - 1:LANES]


def _fg_cumsum(fg_t, b_fgate):
    bsz, heads, seq = fg_t.shape
    return pl.pallas_call(
        _fg_cumsum_kernel,
        grid=(bsz,),
        in_specs=[
            pl.BlockSpec((1, heads, seq), lambda b: (b, 0, 0)),
            pl.BlockSpec((heads, 1), lambda b: (0, 0)),
        ],
        out_specs=pl.BlockSpec((1, heads, seq), lambda b: (b, 0, 0)),
        out_shape=jax.ShapeDtypeStruct((bsz, heads, seq), F32),
        compiler_params=_params("parallel"),
    )(fg_t, b_fgate.reshape(heads, 1))


def _attn_kernel(q_ref, k_ref, v_ref, cq_ref, ck_ref, o_ref, *, blk, scale):
    qi = pl.program_id(2)
    q = q_ref[0]
    cq = cq_ref[0, 0]

    def step(j, carry, masked):
        m, l, acc = carry
        start = pl.multiple_of(j * blk, blk)
        kj = k_ref[0, pl.ds(start, blk), :]
        vj = v_ref[0, pl.ds(start, blk), :]
        s = lax.dot_general(q, kj, (((1,), (1,)), ((), ())), preferred_element_type=F32) * scale
        s = s + (cq - ck_ref[0, 0, j])
        if masked:
            row = lax.broadcasted_iota(jnp.int32, (blk, blk), 0)
            col = lax.broadcasted_iota(jnp.int32, (blk, blk), 1)
            s = jnp.where(col <= row, s, -jnp.inf)
        m_new = jnp.maximum(m, jnp.max(s, axis=-1, keepdims=True))
        alpha = jnp.exp(m - m_new)
        p = jnp.exp(s - m_new)
        l = alpha * l + jnp.sum(p, axis=-1, keepdims=True)
        acc = alpha * acc + _dot(p.astype(BF16), vj)
        return m_new, l, acc

    init = (jnp.full((blk, 1), -jnp.inf, F32), jnp.zeros((blk, 1), F32),
            jnp.zeros((blk, q.shape[-1]), F32))
    carry = lax.fori_loop(0, qi, lambda j, c: step(j, c, False), init)
    _, l, acc = step(qi, carry, True)
    o_ref[0] = (acc / l).astype(o_ref.dtype)


def _fox_attention(qkv, c_t, blk):
    bsz, seq, _ = qkv.shape
    h, dh = ATT_HEADS, ATT_HEAD_DIM
    cq = c_t.reshape(bsz, h, seq, 1)
    ck = c_t.reshape(bsz, h, seq // blk, 1, blk)
    return pl.pallas_call(
        functools.partial(_attn_kernel, blk=blk, scale=dh ** -0.5),
        grid=(bsz, h, seq // blk),
        in_specs=[
            pl.BlockSpec((1, blk, dh), lambda b, hh, i: (b, i, hh)),
            pl.BlockSpec((1, seq, dh), lambda b, hh, i: (b, 0, h + hh)),
            pl.BlockSpec((1, seq, dh), lambda b, hh, i: (b, 0, 2 * h + hh)),
            pl.BlockSpec((1, 1, blk, 1), lambda b, hh, i: (b, hh, i, 0)),
            pl.BlockSpec((1, 1, seq // blk, 1, blk), lambda b, hh, i: (b, hh, 0, 0, 0)),
        ],
        out_specs=pl.BlockSpec((1, blk, dh), lambda b, hh, i: (b, i, hh)),
        out_shape=jax.ShapeDtypeStruct((bsz, seq, h * dh), BF16),
        compiler_params=_params("parallel", "parallel", "arbitrary"),
    )(qkv, qkv, qkv, cq, ck)


def _ssd_kernel(xbc_ref, z_ref, dt_ref, dtt_ref, cw_ref, cb_ref, dtb_ref, dtbt_ref, alog_ref,
                alogt_ref, dskip_ref, gain_ref, expand_ref, y_ref, ext_ref, act_ref, state_ref):
    chunk = SSM_CHUNK
    width = y_ref.shape[2]
    heads = dt_ref.shape[2]
    per_group = heads // SSM_GROUPS
    gw = per_group * SSM_HEAD_DIM
    nstate = SSM_STATE
    halo = SUBLANES
    c_idx = pl.program_id(1)

    @pl.when(c_idx == 0)
    def _():
        ext_ref[0:halo, :] = jnp.zeros((halo, ext_ref.shape[1]), F32)
        state_ref[...] = jnp.zeros(state_ref.shape, F32)

    ext_ref[halo:halo + chunk, :] = xbc_ref[0]
    conv_ch = ext_ref.shape[1]
    col_tile = 4 * LANES
    for ct in range(conv_ch // col_tile):
        sl = slice(ct * col_tile, (ct + 1) * col_tile)
        acc = cb_ref[:, sl] + cw_ref[CONV_WIDTH - 1:CONV_WIDTH, sl] * ext_ref[halo:halo + chunk, sl]
        for kk in range(CONV_WIDTH - 1):
            off = halo - (CONV_WIDTH - 1) + kk
            acc = acc + cw_ref[kk:kk + 1, sl] * ext_ref[off:off + chunk, sl]
        act_ref[:, sl] = acc * _sigmoid(acc)
    ext_ref[0:halo, :] = xbc_ref[0, chunk - halo:chunk, :]

    row = lax.broadcasted_iota(jnp.int32, (chunk, chunk), 0)
    col = lax.broadcasted_iota(jnp.int32, (chunk, chunk), 1)
    causal = row >= col
    tril = causal.astype(BF16)
    triu = (row <= col).astype(BF16)
    dt = _softplus(dt_ref[0] + dtb_ref[...])
    adt = dt * (-jnp.exp(alog_ref[...]))
    acs = _dot_exact_lhs(tril, adt)
    dtt = _softplus(dtt_ref[0] + dtbt_ref[...])
    acs_t = _dot_exact_rhs(dtt * (-jnp.exp(alogt_ref[...])), triu)

    expand = expand_ref[...]
    dt_x = _dot_exact_rhs(dt, expand)
    acs_x = _dot_exact_rhs(acs, expand)
    last_x = acs_x[chunk - 1:chunk, :]
    xs = act_ref[:, 0:width]
    xc = xs * dt_x
    xc_bf = xc.astype(BF16)
    xcd_bf = (xc * jnp.exp(last_x - acs_x)).astype(BF16)
    in_decay = jnp.exp(acs_x)
    chunk_decay = jnp.exp(last_x)
    lane = lax.broadcasted_iota(jnp.int32, (chunk, LANES), 1)
    low_half = lane < SSM_HEAD_DIM

    for g in range(SSM_GROUPS):
        gsl = slice(g * gw, (g + 1) * gw)
        b_g = act_ref[:, width + g * nstate:width + (g + 1) * nstate]
        c_g = act_ref[:, width + (SSM_GROUPS + g) * nstate:width + (SSM_GROUPS + g + 1) * nstate]
        c_bf = c_g.astype(BF16)
        cb = lax.dot_general(c_bf, b_g.astype(BF16), (((1,), (1,)), ((), ())),
                             preferred_element_type=F32)
        pieces = []
        for pair in range(per_group // 2):
            ms = []
            for r in (2 * pair, 2 * pair + 1):
                hd = g * per_group + r
                seg = acs[:, hd:hd + 1] - acs_t[hd:hd + 1, :]
                decay = jnp.exp(jnp.where(causal, seg, -jnp.inf))
                ms.append((cb * decay).astype(BF16))
            xpair = xc_bf[:, g * gw + pair * LANES:g * gw + (pair + 1) * LANES]
            zero = jnp.zeros_like(xpair)
            rhs = jnp.concatenate([jnp.where(low_half, xpair, zero),
                                   jnp.where(low_half, zero, xpair)], axis=0)
            pieces.append(_dot(jnp.concatenate(ms, axis=1), rhs))
        y_diag = jnp.concatenate(pieces, axis=1)
        state = state_ref[g]
        y_off = _dot(c_bf, state.astype(BF16)) * in_decay[:, gsl]
        y = y_diag + y_off + dskip_ref[:, gsl] * xs[:, gsl]
        zg = z_ref[0, :, gsl]
        y = y * (zg * _sigmoid(zg))
        y = y * lax.rsqrt(jnp.mean(y * y, axis=-1, keepdims=True) + EPS) * gain_ref[:, gsl]
        y_ref[0, :, gsl] = y.astype(y_ref.dtype)
        state_ref[g] = state * chunk_decay[:, gsl] + _dot(b_g.T.astype(BF16), xcd_bf[:, gsl])


def _ssd(xbc, z, dt_raw, dt_raw_t, conv_w, conv_b, dt_bias, a_log, d_skip, gain):
    bsz, seq, conv_ch = xbc.shape
    width = z.shape[2]
    heads = dt_raw.shape[2]
    chunk = SSM_CHUNK
    gw = width // SSM_GROUPS
    expand = (jnp.arange(width)[None, :] // SSM_HEAD_DIM == jnp.arange(heads)[:, None]).astype(BF16)
    dskip_x = jnp.repeat(d_skip, SSM_HEAD_DIM).reshape(1, width)
    const = lambda shape: pl.BlockSpec(shape, lambda b, c: (0,) * len(shape))
    return pl.pallas_call(
        _ssd_kernel,
        grid=(bsz, seq // chunk),
        in_specs=[
            pl.BlockSpec((1, chunk, conv_ch), lambda b, c: (b, c, 0)),
            pl.BlockSpec((1, chunk, width), lambda b, c: (b, c, 0)),
            pl.BlockSpec((1, chunk, heads), lambda b, c: (b, c, 0)),
            pl.BlockSpec((1, heads, chunk), lambda b, c: (b, 0, c)),
            const((CONV_WIDTH, conv_ch)),
            const((1, conv_ch)),
            const((1, heads)),
            const((heads, 1)),
            const((1, heads)),
            const((heads, 1)),
            const((1, width)),
            const((1, width)),
            const((heads, width)),
        ],
        out_specs=pl.BlockSpec((1, chunk, width), lambda b, c: (b, c, 0)),
        out_shape=jax.ShapeDtypeStruct((bsz, seq, width), BF16),
        scratch_shapes=[
            pltpu.VMEM((SUBLANES + chunk, conv_ch), F32),
            pltpu.VMEM((chunk, conv_ch), F32),
            pltpu.VMEM((SSM_GROUPS, SSM_STATE, gw), F32),
        ],
        compiler_params=_params("parallel", "arbitrary"),
    )(xbc, z, dt_raw, dt_raw_t, conv_w, conv_b.reshape(1, conv_ch), dt_bias.reshape(1, heads),
      dt_bias.reshape(heads, 1), a_log.reshape(1, heads), a_log.reshape(heads, 1), dskip_x,
      gain.reshape(1, width), expand)


def _out_proj_kernel(att_ref, y_ref, x_ref, ga_ref, wa_ref, wy_ref, gpost_ref, gpre_ref, wr_ref,
                     br_ref, h_ref, hn_ref, logit_ref):
    att_n = _rms(att_ref[...].astype(F32), ga_ref[...]).astype(BF16)
    mixed = _dot(att_n, wa_ref[...]) + _dot(y_ref[...], wy_ref[...])
    h = x_ref[...] + _rms(mixed, gpost_ref[...])
    h_ref[...] = h
    hn = _rms(h, gpre_ref[...])
    hn_ref[...] = hn
    logit_ref[...] = jnp.dot(hn, wr_ref[...], preferred_element_type=F32,
                             precision=lax.Precision.HIGHEST) + br_ref[...]


def _out_proj(att, y, x, attn_gain, w_att, w_y, post_gain, pre_ffn_gain, w_router, b_router, tm):
    t, d = x.shape
    ne = w_router.shape[1]
    rows = lambda width: pl.BlockSpec((tm, width), lambda i: (i, 0))
    whole = lambda a: pl.BlockSpec(a.shape, lambda i: (0, 0), pipeline_mode=pl.Buffered(1))
    args = (att, y, x, attn_gain.reshape(1, -1), w_att, w_y, post_gain.reshape(1, d),
            pre_ffn_gain.reshape(1, d), w_router, b_router.reshape(1, ne))
    return pl.pallas_call(
        _out_proj_kernel,
        grid=(t // tm,),
        in_specs=[rows(att.shape[1]), rows(y.shape[1]), rows(d)] + [whole(a) for a in args[3:]],
        out_specs=[rows(d), rows(d), rows(ne)],
        out_shape=[jax.ShapeDtypeStruct((t, d), F32), jax.ShapeDtypeStruct((t, d), F32),
                   jax.ShapeDtypeStruct((t, ne), F32)],
        compiler_params=_params("parallel"),
    )(*args)


def _topk_kernel(logit_ref, e_ref, w_ref):
    l = logit_ref[...]
    ne = l.shape[1]
    lane = lax.broadcasted_iota(jnp.int32, l.shape, 1)
    slot = lax.broadcasted_iota(jnp.int32, e_ref.shape, 1)
    e_out = jnp.zeros(e_ref.shape, jnp.int32)
    p_out = jnp.zeros(w_ref.shape, F32)
    top = None
    for kk in range(TOP_K):
        m = jnp.max(l, axis=-1, keepdims=True)
        idx = jnp.min(jnp.where(l == m, lane, ne), axis=-1, keepdims=True)
        if top is None:
            top = m
        e_out = jnp.where(slot == kk, idx, e_out)
        p_out = jnp.where(slot == kk, jnp.exp(m - top), p_out)
        l = jnp.where(lane == idx, -jnp.inf, l)
    e_ref[...] = e_out
    w_ref[...] = p_out / jnp.sum(p_out, axis=-1, keepdims=True)


def _router_topk(logits, tm):
    t, ne = logits.shape
    return pl.pallas_call(
        _topk_kernel,
        grid=(t // tm,),
        in_specs=[pl.BlockSpec((tm, ne), lambda i: (i, 0))],
        out_specs=[pl.BlockSpec((tm, TOP_K), lambda i: (i, 0))] * 2,
        out_shape=[jax.ShapeDtypeStruct((t, TOP_K), jnp.int32),
                   jax.ShapeDtypeStruct((t, TOP_K), F32)],
        compiler_params=_params("parallel"),
    )(logits)


def _gather_kernel(idx_ref, src_ref, o_ref, buf_ref, sem):
    rows = buf_ref.shape[0]

    def row_copy(r):
        return pltpu.make_async_copy(src_ref.at[pl.ds(idx_ref[0, 0, r], 1)],
                                     buf_ref.at[pl.ds(r, 1)], sem)

    def start(r, carry):
        row_copy(r).start()
        return carry

    def wait(r, carry):
        row_copy(r).wait()
        return carry

    lax.fori_loop(0, rows, start, 0)
    lax.fori_loop(0, rows, wait, 0)
    o_ref[...] = buf_ref[...].astype(o_ref.dtype)


def _gather_rows(src, idx, tm, out_dtype):
    d = src.shape[1]
    n_blocks = idx.shape[0] // tm
    return pl.pallas_call(
        _gather_kernel,
        grid=(n_blocks,),
        in_specs=[
            pl.BlockSpec((1, 1, tm), lambda i: (i, 0, 0), memory_space=pltpu.SMEM),
            pl.BlockSpec(memory_space=pl.ANY),
        ],
        out_specs=pl.BlockSpec((tm, d), lambda i: (i, 0)),
        out_shape=jax.ShapeDtypeStruct((n_blocks * tm, d), out_dtype),
        scratch_shapes=[pltpu.VMEM((tm, d), src.dtype), pltpu.SemaphoreType.DMA(())],
        compiler_params=_params("arbitrary"),
    )(idx.reshape(n_blocks, 1, tm), src)


def _expert_changed(be_ref, i):
    return jnp.logical_or(i == 0, be_ref[i] != be_ref[jnp.maximum(i - 1, 0)])


def _moe_up_kernel(be_ref, used_ref, x_ref, wg_ref, wu_ref, bg_ref, bu_ref, h_ref, wg_bf, wu_bf):
    i = pl.program_id(1)

    @pl.when(_expert_changed(be_ref, i))
    def _():
        wg_bf[...] = wg_ref[0].astype(BF16)
        wu_bf[...] = wu_ref[0].astype(BF16)

    @pl.when(i < used_ref[0])
    def _():
        x = x_ref[...]
        gate = jnp.minimum(_dot(x, wg_bf[...]) + bg_ref[0], SWIGLU_LIMIT)
        up = jnp.clip(_dot(x, wu_bf[...]) + bu_ref[0], -SWIGLU_LIMIT, SWIGLU_LIMIT)
        h_ref[...] = ((up + 1.0) * (gate * _sigmoid(gate * SWIGLU_ALPHA))).astype(h_ref.dtype)

    @pl.when(i >= used_ref[0])
    def _():
        h_ref[...] = jnp.zeros(h_ref.shape, h_ref.dtype)


def _moe_down_kernel(be_ref, used_ref, h_ref, wd_ref, bd_ref, y_ref, wd_bf):
    i = pl.program_id(1)

    @pl.when(_expert_changed(be_ref, i))
    def _():
        wd_bf[...] = wd_ref[0].astype(BF16)

    @pl.when(i < used_ref[0])
    def _():
        y_ref[...] = _dot(h_ref[...], wd_bf[...]) + bd_ref[0]

    @pl.when(i >= used_ref[0])
    def _():
        y_ref[...] = jnp.zeros(y_ref.shape, y_ref.dtype)


def _moe_up(xs, block_e, n_used, w_gate, b_gate, w_up, b_up, tm, tn):
    n_pad, d = xs.shape
    ne, _, ff = w_gate.shape
    w_spec = pl.BlockSpec((1, d, tn), lambda j, i, be, nu: (be[i], 0, j))
    b_spec = pl.BlockSpec((1, 1, tn), lambda j, i, be, nu: (be[i], 0, j))
    return pl.pallas_call(
        _moe_up_kernel,
        grid_spec=pltpu.PrefetchScalarGridSpec(
            num_scalar_prefetch=2,
            grid=(ff // tn, n_pad // tm),
            in_specs=[pl.BlockSpec((tm, d), lambda j, i, be, nu: (i, 0)),
                      w_spec, w_spec, b_spec, b_spec],
            out_specs=pl.BlockSpec((tm, tn), lambda j, i, be, nu: (i, j)),
            scratch_shapes=[pltpu.VMEM((d, tn), BF16), pltpu.VMEM((d, tn), BF16)],
        ),
        out_shape=jax.ShapeDtypeStruct((n_pad, ff), BF16),
        compiler_params=_params("arbitrary", "arbitrary"),
    )(block_e, n_used, xs, w_gate, w_up, b_gate.reshape(ne, 1, ff), b_up.reshape(ne, 1, ff))


def _moe_down(hs, block_e, n_used, w_down, b_down, tm, tn):
    n_pad, ff = hs.shape
    ne, _, d = w_down.shape
    return pl.pallas_call(
        _moe_down_kernel,
        grid_spec=pltpu.PrefetchScalarGridSpec(
            num_scalar_prefetch=2,
            grid=(d // tn, n_pad // tm),
            in_specs=[pl.BlockSpec((tm, ff), lambda j, i, be, nu: (i, 0)),
                      pl.BlockSpec((1, ff, tn), lambda j, i, be, nu: (be[i], 0, j)),
                      pl.BlockSpec((1, 1, tn), lambda j, i, be, nu: (be[i], 0, j))],
            out_specs=pl.BlockSpec((tm, tn), lambda j, i, be, nu: (i, j)),
            scratch_shapes=[pltpu.VMEM((ff, tn), BF16)],
        ),
        out_shape=jax.ShapeDtypeStruct((n_pad, d), F32),
        compiler_params=_params("arbitrary", "arbitrary"),
    )(block_e, n_used, hs, w_down, b_down.reshape(ne, 1, d))


def _combine_ple_kernel(pos_ref, y_ref, w_ref, h_ref, p_ref, gffn_ref, gple_ref, wgate_ref,
                        wproj_ref, gpost_ref, o_ref, buf_ref, sem):
    tm = h_ref.shape[0]

    def row_copy(r):
        kk = r // tm
        tok = r - kk * tm
        return pltpu.make_async_copy(y_ref.at[pl.ds(pos_ref[0, 0, r], 1)],
                                     buf_ref.at[kk, pl.ds(tok, 1)], sem)

    def start(r, carry):
        row_copy(r).start()
        return carry

    def wait(r, carry):
        row_copy(r).wait()
        return carry

    lax.fori_loop(0, TOP_K * tm, start, 0)
    lax.fori_loop(0, TOP_K * tm, wait, 0)
    w = w_ref[...]
    ff = w[:, 0:1] * buf_ref[0]
    for kk in range(1, TOP_K):
        ff = ff + w[:, kk:kk + 1] * buf_ref[kk]
    h = h_ref[...] + _rms(ff, gffn_ref[...])
    gate = _sigmoid(_dot(_rms(h, gple_ref[...]).astype(BF16), wgate_ref[...]))
    emb = _dot(p_ref[...].astype(BF16), wproj_ref[...])
    o_ref[...] = h + _rms(gate * emb, gpost_ref[...])


def _combine_ple(y_rows, pos, top_w, h, p, ffn_gain, ple_gain, w_ple_gate, w_ple_proj,
                 ple_post_gain, tm):
    t, d = h.shape
    rows = lambda width: pl.BlockSpec((tm, width), lambda i: (i, 0))
    whole = lambda a: pl.BlockSpec(a.shape, lambda i: (0, 0), pipeline_mode=pl.Buffered(1))
    consts = (ffn_gain.reshape(1, d), ple_gain.reshape(1, d), w_ple_gate, w_ple_proj,
              ple_post_gain.reshape(1, d))
    return pl.pallas_call(
        _combine_ple_kernel,
        grid=(t // tm,),
        in_specs=[
            pl.BlockSpec((1, 1, TOP_K * tm), lambda i: (i, 0, 0), memory_space=pltpu.SMEM),
            pl.BlockSpec(memory_space=pl.ANY),
            rows(TOP_K), rows(d), rows(p.shape[1]),
        ] + [whole(a) for a in consts],
        out_specs=rows(d),
        out_shape=jax.ShapeDtypeStruct((t, d), F32),
        scratch_shapes=[pltpu.VMEM((TOP_K, tm, d), F32), pltpu.SemaphoreType.DMA(())],
        compiler_params=_params("arbitrary"),
    )(pos, y_rows, top_w, h, p, *consts)


def _routing_tables(top_e, tm, tok_block):
    t = top_e.shape[0]
    n = t * TOP_K
    e_flat = top_e.reshape(n)
    order = jnp.argsort(e_flat, stable=True)
    e_sorted = e_flat[order]
    counts = jnp.bincount(e_flat, length=N_EXPERTS)
    group_start = jnp.cumsum(counts) - counts
    padded = ((counts + tm - 1) // tm) * tm
    padded_end = jnp.cumsum(padded)
    padded_start = padded_end - padded
    dest = (padded_start[e_sorted] + (jnp.arange(n) - group_start[e_sorted])).astype(jnp.int32)
    n_blocks = n // tm + N_EXPERTS
    src_tok = jnp.zeros((n_blocks * tm,), jnp.int32).at[dest].set((order // TOP_K).astype(jnp.int32))
    pos = jnp.zeros((n,), jnp.int32).at[order].set(dest).reshape(t, TOP_K)
    n_used = (padded_end[-1] // tm).astype(jnp.int32).reshape(1)
    block_e = jnp.searchsorted(padded_end, jnp.arange(n_blocks) * tm, side='right')
    last_e = block_e[jnp.maximum(n_used[0] - 1, 0)]
    block_e = jnp.where(jnp.arange(n_blocks) < n_used[0], block_e, last_e).astype(jnp.int32)
    pos_blocks = pos.reshape(t // tok_block, tok_block, TOP_K).transpose(0, 2, 1)
    return src_tok, pos_blocks.reshape(t // tok_block, 1, TOP_K * tok_block), block_e, n_used


def _layer(h, p, pre_mix_norm, w_in, b_fgate, conv_w, conv_b, dt_bias, a_log, d_skip,
           attn_out_norm, ssm_out_norm, w_out, post_mix_norm, pre_ffn_norm, w_router, b_router,
           w_gate, b_gate, w_up, b_up, w_down, b_down, post_ffn_norm, ple_norm, w_ple_gate,
           w_ple_proj, ple_post_norm):
    bsz, seq, d = h.shape
    t = bsz * seq
    att_w = ATT_HEADS * ATT_HEAD_DIM
    heads = dt_bias.shape[0]
    ssm_w = heads * SSM_HEAD_DIM
    conv_ch = conv_w.shape[1]
    o_fg = 3 * att_w
    o_z = o_fg + ATT_HEADS
    o_xbc = o_z + ssm_w
    o_dt = o_xbc + conv_ch
    x2 = h.reshape(t, d)

    w_qkv = w_in[:, :o_fg].astype(BF16)
    w_z = w_in[:, o_z:o_xbc].astype(BF16)
    w_xbc = w_in[:, o_xbc:o_dt].astype(BF16)
    small_pad = LANES - ATT_HEADS - heads
    w_small = jnp.concatenate([w_in[:, o_fg:o_z], w_in[:, o_dt:], jnp.zeros((d, small_pad), F32)], axis=1)

    tm = min(512, t)
    qkv = _rms_matmul(x2, pre_mix_norm, w_qkv, BF16, tm, 1024)
    zz = _rms_matmul(x2, pre_mix_norm, w_z, F32, tm, 1024)
    xbc = _rms_matmul(x2, pre_mix_norm, w_xbc, F32, tm, 1024)
    small = _rms_matmul(x2, pre_mix_norm, w_small, F32, tm, LANES).reshape(bsz, seq, LANES)

    fg_t = jnp.swapaxes(small[:, :, :ATT_HEADS], 1, 2)
    dt_raw = small[:, :, ATT_HEADS:ATT_HEADS + heads]
    c_t = _fg_cumsum(fg_t, b_fgate)
    att = _fox_attention(qkv.reshape(bsz, seq, 3 * att_w), c_t, min(512, seq))
    y = _ssd(xbc.reshape(bsz, seq, conv_ch), zz.reshape(bsz, seq, ssm_w), dt_raw,
             jnp.swapaxes(dt_raw, 1, 2), conv_w, conv_b, dt_bias, a_log, d_skip, ssm_out_norm)

    w_out_bf = w_out.astype(BF16)
    h1, hn, logits = _out_proj(att.reshape(t, att_w), y.reshape(t, ssm_w), x2, attn_out_norm,
                               w_out_bf[:att_w], w_out_bf[att_w:], post_mix_norm, pre_ffn_norm,
                               w_router, b_router, min(256, t))
    top_e, top_w = _router_topk(logits, min(2048, t))

    tok_block = min(256, t)
    src_tok, pos, block_e, n_used = _routing_tables(top_e, MOE_ROWS, tok_block)
    xs = _gather_rows(hn, src_tok, MOE_ROWS, BF16)
    hs = _moe_up(xs, block_e, n_used, w_gate, b_gate, w_up, b_up, MOE_ROWS, min(512, w_gate.shape[2]))
    y_rows = _moe_down(hs, block_e, n_used, w_down, b_down, MOE_ROWS, min(512, d))
    out = _combine_ple(y_rows, pos, top_w, h1, p.reshape(t, -1), post_ffn_norm, ple_norm,
                       w_ple_gate.astype(BF16), w_ple_proj.astype(BF16), ple_post_norm, tok_block)
    return out.reshape(bsz, seq, d)


def kernel(x, p, pre_mix_norm, w_in, b_fgate, conv_w, conv_b, dt_bias, a_log, d_skip, attn_out_norm, ssm_out_norm, w_out, post_mix_norm, pre_ffn_norm, w_router, b_router, w_gate, b_gate, w_up, b_up, w_down, b_down, post_ffn_norm, ple_norm, w_ple_gate, w_ple_proj, ple_post_norm):
    h = x
    for i in range(p.shape[0]):
        h = _layer(h, p[i], pre_mix_norm[i], w_in[i], b_fgate[i], conv_w[i], conv_b[i], dt_bias[i],
                   a_log[i], d_skip[i], attn_out_norm[i], ssm_out_norm[i], w_out[i],
                   post_mix_norm[i], pre_ffn_norm[i], w_router[i], b_router[i], w_gate[i], b_gate[i],
                   w_up[i], b_up[i], w_down[i], b_down[i], post_ffn_norm[i], ple_norm[i],
                   w_ple_gate[i], w_ple_proj[i], ple_post_norm[i])
    return h
```

```python
import functools

import jax
import jax.numpy as jnp
from jax import lax
from jax.experimental import pallas as pl
from jax.experimental.pallas import tpu as pltpu

F32 = jnp.float32
BF16 = jnp.bfloat16

ATT_HEADS = 8
ATT_HEAD_DIM = 128
SSM_HEAD_DIM = 64
SSM_GROUPS = 8
SSM_STATE = 128
SSM_CHUNK = 128
CONV_WIDTH = 4
N_EXPERTS = 32
TOP_K = 4
SWIGLU_LIMIT = 7.0
SWIGLU_ALPHA = 1.702
EPS = 1e-6

LANES = 128
SUBLANES = 8
VMEM_LIMIT_BYTES = 56 * 1024 * 1024

MOE_ROWS = 512


def _params(*semantics):
    return pltpu.CompilerParams(dimension_semantics=semantics, vmem_limit_bytes=VMEM_LIMIT_BYTES)


def _rms(x, gain):
    return x * lax.rsqrt(jnp.mean(x * x, axis=-1, keepdims=True) + EPS) * gain


def _sigmoid(x):
    return 1.0 / (1.0 + jnp.exp(-x))


def _softplus(x):
    return jnp.maximum(x, 0.0) + jnp.log1p(jnp.exp(-jnp.abs(x)))


def _split3(x):
    x1 = x.astype(BF16)
    r1 = x - x1.astype(F32)
    x2 = r1.astype(BF16)
    x3 = (r1 - x2.astype(F32)).astype(BF16)
    return x1, x2, x3


def _dot(a, b):
    return jnp.dot(a, b, preferred_element_type=F32)


def _dot_exact_rhs(x, m):
    x1, x2, x3 = _split3(x)
    return _dot(x1, m) + _dot(x2, m) + _dot(x3, m)


def _dot_exact_lhs(m, x):
    x1, x2, x3 = _split3(x)
    return _dot(m, x1) + _dot(m, x2) + _dot(m, x3)


def _store_slabs(ref, base, value):
    rows, width = value.shape
    n = width // LANES
    for c in range(n):
        ref[pl.ds(base + c, rows, stride=n), :] = value[:, c * LANES:(c + 1) * LANES]


def _load_slabs(ref, base, rows, n):
    return jnp.concatenate([ref[pl.ds(base + c, rows, stride=n), :] for c in range(n)], axis=1)


def _rms_matmul_kernel(x_ref, g_ref, w_ref, o_ref, xn_ref):
    @pl.when(pl.program_id(1) == 0)
    def _():
        xn_ref[...] = _rms(x_ref[...], g_ref[...]).astype(xn_ref.dtype)

    if xn_ref.dtype == F32:
        acc = jnp.dot(xn_ref[...], w_ref[...], preferred_element_type=F32,
                      precision=lax.Precision.HIGHEST)
    else:
        acc = _dot(xn_ref[...], w_ref[...])
    o_ref[...] = acc.astype(o_ref.dtype)


def _rms_matmul(x, gain, w, out_dtype, tm, tn):
    m, k = x.shape
    n = w.shape[1]
    return pl.pallas_call(
        _rms_matmul_kernel,
        grid=(m // tm, n // tn),
        in_specs=[
            pl.BlockSpec((tm, k), lambda i, j: (i, 0)),
            pl.BlockSpec((1, k), lambda i, j: (0, 0)),
            pl.BlockSpec((k, tn), lambda i, j: (0, j)),
        ],
        out_specs=pl.BlockSpec((tm, tn), lambda i, j: (i, j)),
        out_shape=jax.ShapeDtypeStruct((m, n), out_dtype),
        scratch_shapes=[pltpu.VMEM((tm, k), w.dtype)],
        compiler_params=_params("parallel", "arbitrary"),
    )(x, gain.reshape(1, k), w)


def _fg_cumsum_kernel(fg_ref, b_ref, c_ref):
    seq = fg_ref.shape[2]
    row = lax.broadcasted_iota(jnp.int32, (LANES, LANES), 0)
    col = lax.broadcasted_iota(jnp.int32, (LANES, LANES), 1)
    triu = (row <= col).astype(BF16)
    carry = jnp.zeros((fg_ref.shape[1], 1), F32)
    for ci in range(seq // LANES):
        z = fg_ref[0, :, ci * LANES:(ci + 1) * LANES] + b_ref[...]
        ls = jnp.minimum(z, 0.0) - jnp.log1p(jnp.exp(-jnp.abs(z)))
        cs = _dot_exact_rhs(ls, triu) + carry
        c_ref[0, :, ci * LANES:(ci + 1) * LANES] = cs
        carry = cs[:, LANES - 1:LANES]


def _fg_cumsum(fg_t, b_fgate):
    bsz, heads, seq = fg_t.shape
    return pl.pallas_call(
        _fg_cumsum_kernel,
        grid=(bsz,),
        in_specs=[
            pl.BlockSpec((1, heads, seq), lambda b: (b, 0, 0)),
            pl.BlockSpec((heads, 1), lambda b: (0, 0)),
        ],
        out_specs=pl.BlockSpec((1, heads, seq), lambda b: (b, 0, 0)),
        out_shape=jax.ShapeDtypeStruct((bsz, heads, seq), F32),
        compiler_params=_params("parallel"),
    )(fg_t, b_fgate.reshape(heads, 1))


def _attn_kernel(q_ref, k_ref, v_ref, cq_ref, ck_ref, o_ref, *, blk, scale):
    qi = pl.program_id(2)
    q = q_ref[0]
    cq = cq_ref[0, 0]

    def step(j, carry, masked):
        m, l, acc = carry
        start = pl.multiple_of(j * blk, blk)
        kj = k_ref[0, pl.ds(start, blk), :]
        vj = v_ref[0, pl.ds(start, blk), :]
        s = lax.dot_general(q, kj, (((1,), (1,)), ((), ())), preferred_element_type=F32) * scale
        s = s + (cq - ck_ref[0, 0, j])
        if masked:
            row = lax.broadcasted_iota(jnp.int32, (blk, blk), 0)
            col = lax.broadcasted_iota(jnp.int32, (blk, blk), 1)
            s = jnp.where(col <= row, s, -jnp.inf)
        m_new = jnp.maximum(m, jnp.max(s, axis=-1, keepdims=True))
        alpha = jnp.exp(m - m_new)
        p = jnp.exp(s - m_new)
        l = alpha * l + jnp.sum(p, axis=-1, keepdims=True)
        acc = alpha * acc + _dot(p.astype(BF16), vj)
        return m_new, l, acc

    init = (jnp.full((blk, 1), -jnp.inf, F32), jnp.zeros((blk, 1), F32),
            jnp.zeros((blk, q.shape[-1]), F32))
    carry = lax.fori_loop(0, qi, lambda j, c: step(j, c, False), init)
    _, l, acc = step(qi, carry, True)
    o_ref[0] = (acc / l).astype(o_ref.dtype)


def _fox_attention(qkv, c_t, blk):
    bsz, seq, _ = qkv.shape
    h, dh = ATT_HEADS, ATT_HEAD_DIM
    cq = c_t.reshape(bsz, h, seq, 1)
    ck = c_t.reshape(bsz, h, seq // blk, 1, blk)
    return pl.pallas_call(
        functools.partial(_attn_kernel, blk=blk, scale=dh ** -0.5),
        grid=(bsz, h, seq // blk),
        in_specs=[
            pl.BlockSpec((1, blk, dh), lambda b, hh, i: (b, i, hh)),
            pl.BlockSpec((1, seq, dh), lambda b, hh, i: (b, 0, h + hh)),
            pl.BlockSpec((1, seq, dh), lambda b, hh, i: (b, 0, 2 * h + hh)),
            pl.BlockSpec((1, 1, blk, 1), lambda b, hh, i: (b, hh, i, 0)),
            pl.BlockSpec((1, 1, seq // blk, 1, blk), lambda b, hh, i: (b, hh, 0, 0, 0)),
        ],
        out_specs=pl.BlockSpec((1, blk, dh), lambda b, hh, i: (b, i, hh)),
        out_shape=jax.ShapeDtypeStruct((bsz, seq, h * dh), BF16),
        compiler_params=_params("parallel", "parallel", "arbitrary"),
    )(qkv, qkv, qkv, cq, ck)


def _ssd_kernel(xbc_ref, z_ref, dt_ref, dtt_ref, cw_ref, cb_ref, dtb_ref, dtbt_ref, alog_ref,
                alogt_ref, dskip_ref, gain_ref, expand_ref, y_ref, ext_ref, act_ref, state_ref):
    chunk = SSM_CHUNK
    width = y_ref.shape[2]
    heads = dt_ref.shape[2]
    per_group = heads // SSM_GROUPS
    gw = per_group * SSM_HEAD_DIM
    nstate = SSM_STATE
    halo = SUBLANES
    c_idx = pl.program_id(1)

    @pl.when(c_idx == 0)
    def _():
        ext_ref[0:halo, :] = jnp.zeros((halo, ext_ref.shape[1]), F32)
        state_ref[...] = jnp.zeros(state_ref.shape, F32)

    ext_ref[halo:halo + chunk, :] = xbc_ref[0]
    conv_ch = ext_ref.shape[1]
    col_tile = 4 * LANES
    for ct in range(conv_ch // col_tile):
        sl = slice(ct * col_tile, (ct + 1) * col_tile)
        acc = cb_ref[:, sl] + cw_ref[CONV_WIDTH - 1:CONV_WIDTH, sl] * ext_ref[halo:halo + chunk, sl]
        for kk in range(CONV_WIDTH - 1):
            off = halo - (CONV_WIDTH - 1) + kk
            acc = acc + cw_ref[kk:kk + 1, sl] * ext_ref[off:off + chunk, sl]
        act_ref[:, sl] = acc * _sigmoid(acc)
    ext_ref[0:halo, :] = xbc_ref[0, chunk - halo:chunk, :]

    row = lax.broadcasted_iota(jnp.int32, (chunk, chunk), 0)
    col = lax.broadcasted_iota(jnp.int32, (chunk, chunk), 1)
    causal = row >= col
    tril = causal.astype(BF16)
    triu = (row <= col).astype(BF16)
    dt = _softplus(dt_ref[0] + dtb_ref[...])
    adt = dt * (-jnp.exp(alog_ref[...]))
    acs = _dot_exact_lhs(tril, adt)
    dtt = _softplus(dtt_ref[0] + dtbt_ref[...])
    acs_t = _dot_exact_rhs(dtt * (-jnp.exp(alogt_ref[...])), triu)

    expand = expand_ref[...]
    dt_x = _dot_exact_rhs(dt, expand)
    acs_x = _dot_exact_rhs(acs, expand)
    last_x = acs_x[chunk - 1:chunk, :]
    xs = act_ref[:, 0:width]
    xc = xs * dt_x
    xc_bf = xc.astype(BF16)
    xcd_bf = (xc * jnp.exp(last_x - acs_x)).astype(BF16)
    in_decay = jnp.exp(acs_x)
    chunk_decay = jnp.exp(last_x)
    lane = lax.broadcasted_iota(jnp.int32, (chunk, LANES), 1)
    low_half = lane < SSM_HEAD_DIM

    for g in range(SSM_GROUPS):
        gsl = slice(g * gw, (g + 1) * gw)
        b_g = act_ref[:, width + g * nstate:width + (g + 1) * nstate]
        c_g = act_ref[:, width + (SSM_GROUPS + g) * nstate:width + (SSM_GROUPS + g + 1) * nstate]
        c_bf = c_g.astype(BF16)
        cb = lax.dot_general(c_bf, b_g.astype(BF16), (((1,), (1,)), ((), ())),
                             preferred_element_type=F32)
        pieces = []
        for pair in range(per_group // 2):
            ms = []
            for r in (2 * pair, 2 * pair + 1):
                hd = g * per_group + r
                seg = acs[:, hd:hd + 1] - acs_t[hd:hd + 1, :]
                decay = jnp.exp(jnp.where(causal, seg, -jnp.inf))
                ms.append((cb * decay).astype(BF16))
            xpair = xc_bf[:, g * gw + pair * LANES:g * gw + (pair + 1) * LANES]
            zero = jnp.zeros_like(xpair)
            rhs = jnp.concatenate([jnp.where(low_half, xpair, zero),
                                   jnp.where(low_half, zero, xpair)], axis=0)
            pieces.append(_dot(jnp.concatenate(ms, axis=1), rhs))
        y_diag = jnp.concatenate(pieces, axis=1)
        state = state_ref[g]
        y_off = _dot(c_bf, state.astype(BF16)) * in_decay[:, gsl]
        y = y_diag + y_off + dskip_ref[:, gsl] * xs[:, gsl]
        zg = z_ref[0, :, gsl]
        y = y * (zg * _sigmoid(zg))
        y = y * lax.rsqrt(jnp.mean(y * y, axis=-1, keepdims=True) + EPS) * gain_ref[:, gsl]
        y_ref[0, :, gsl] = y.astype(y_ref.dtype)
        state_ref[g] = state * chunk_decay[:, gsl] + _dot(b_g.T.astype(BF16), xcd_bf[:, gsl])


def _ssd(xbc, z, dt_raw, dt_raw_t, conv_w, conv_b, dt_bias, a_log, d_skip, gain):
    bsz, seq, conv_ch = xbc.shape
    width = z.shape[2]
    heads = dt_raw.shape[2]
    chunk = SSM_CHUNK
    gw = width // SSM_GROUPS
    expand = (jnp.arange(width)[None, :] // SSM_HEAD_DIM == jnp.arange(heads)[:, None]).astype(BF16)
    dskip_x = jnp.repeat(d_skip, SSM_HEAD_DIM).reshape(1, width)
    const = lambda shape: pl.BlockSpec(shape, lambda b, c: (0,) * len(shape))
    return pl.pallas_call(
        _ssd_kernel,
        grid=(bsz, seq // chunk),
        in_specs=[
            pl.BlockSpec((1, chunk, conv_ch), lambda b, c: (b, c, 0)),
            pl.BlockSpec((1, chunk, width), lambda b, c: (b, c, 0)),
            pl.BlockSpec((1, chunk, heads), lambda b, c: (b, c, 0)),
            pl.BlockSpec((1, heads, chunk), lambda b, c: (b, 0, c)),
            const((CONV_WIDTH, conv_ch)),
            const((1, conv_ch)),
            const((1, heads)),
            const((heads, 1)),
            const((1, heads)),
            const((heads, 1)),
            const((1, width)),
            const((1, width)),
            const((heads, width)),
        ],
        out_specs=pl.BlockSpec((1, chunk, width), lambda b, c: (b, c, 0)),
        out_shape=jax.ShapeDtypeStruct((bsz, seq, width), BF16),
        scratch_shapes=[
            pltpu.VMEM((SUBLANES + chunk, conv_ch), F32),
            pltpu.VMEM((chunk, conv_ch), F32),
            pltpu.VMEM((SSM_GROUPS, SSM_STATE, gw), F32),
        ],
        compiler_params=_params("parallel", "arbitrary"),
    )(xbc, z, dt_raw, dt_raw_t, conv_w, conv_b.reshape(1, conv_ch), dt_bias.reshape(1, heads),
      dt_bias.reshape(heads, 1), a_log.reshape(1, heads), a_log.reshape(heads, 1), dskip_x,
      gain.reshape(1, width), expand)


def _out_proj_kernel(att_ref, y_ref, x_ref, ga_ref, wa_ref, wy_ref, gpost_ref, gpre_ref, wr_ref,
                     br_ref, h_ref, hn_ref, logit_ref):
    att_n = _rms(att_ref[...].astype(F32), ga_ref[...]).astype(BF16)
    mixed = _dot(att_n, wa_ref[...]) + _dot(y_ref[...], wy_ref[...])
    h = x_ref[...] + _rms(mixed, gpost_ref[...])
    h_ref[...] = h
    hn = _rms(h, gpre_ref[...])
    _store_slabs(hn_ref, 0, hn)
    logit_ref[...] = jnp.dot(hn, wr_ref[...], preferred_element_type=F32,
                             precision=lax.Precision.HIGHEST) + br_ref[...]


def _out_proj(att, y, x, attn_gain, w_att, w_y, post_gain, pre_ffn_gain, w_router, b_router, tm):
    t, d = x.shape
    ne = w_router.shape[1]
    rows = lambda width: pl.BlockSpec((tm, width), lambda i: (i, 0))
    whole = lambda a: pl.BlockSpec(a.shape, lambda i: (0, 0), pipeline_mode=pl.Buffered(1))
    args = (att, y, x, attn_gain.reshape(1, -1), w_att, w_y, post_gain.reshape(1, d),
            pre_ffn_gain.reshape(1, d), w_router, b_router.reshape(1, ne))
    return pl.pallas_call(
        _out_proj_kernel,
        grid=(t // tm,),
        in_specs=[rows(att.shape[1]), rows(y.shape[1]), rows(d)] + [whole(a) for a in args[3:]],
        out_specs=[rows(d), pl.BlockSpec((tm * (d // LANES), LANES), lambda i: (i, 0)), rows(ne)],
        out_shape=[jax.ShapeDtypeStruct((t, d), F32),
                   jax.ShapeDtypeStruct((t * (d // LANES), LANES), F32),
                   jax.ShapeDtypeStruct((t, ne), F32)],
        compiler_params=_params("parallel"),
    )(*args)


def _topk_kernel(logit_ref, e_ref, w_ref):
    l = logit_ref[...]
    ne = l.shape[1]
    lane = lax.broadcasted_iota(jnp.int32, l.shape, 1)
    slot = lax.broadcasted_iota(jnp.int32, e_ref.shape, 1)
    e_out = jnp.zeros(e_ref.shape, jnp.int32)
    p_out = jnp.zeros(w_ref.shape, F32)
    top = None
    for kk in range(TOP_K):
        m = jnp.max(l, axis=-1, keepdims=True)
        idx = jnp.min(jnp.where(l == m, lane, ne), axis=-1, keepdims=True)
        if top is None:
            top = m
        e_out = jnp.where(slot == kk, idx, e_out)
        p_out = jnp.where(slot == kk, jnp.exp(m - top), p_out)
        l = jnp.where(lane == idx, -jnp.inf, l)
    e_ref[...] = e_out
    w_ref[...] = p_out / jnp.sum(p_out, axis=-1, keepdims=True)


def _router_topk(logits, tm):
    t, ne = logits.shape
    return pl.pallas_call(
        _topk_kernel,
        grid=(t // tm,),
        in_specs=[pl.BlockSpec((tm, ne), lambda i: (i, 0))],
        out_specs=[pl.BlockSpec((tm, TOP_K), lambda i: (i, 0))] * 2,
        out_shape=[jax.ShapeDtypeStruct((t, TOP_K), jnp.int32),
                   jax.ShapeDtypeStruct((t, TOP_K), F32)],
        compiler_params=_params("parallel"),
    )(logits)


def _gather_kernel(tok_ref, first_ref, valid_ref, src_ref, o_ref, buf_ref, sem):
    tm = o_ref.shape[0]
    n = buf_ref.shape[0] // tm
    i = pl.program_id(0)
    first = first_ref[i]
    valid = valid_ref[i]
    last = tok_ref.shape[0] - 1

    def start(r, carry):
        tok = jnp.where(r < valid, tok_ref[jnp.clip(first + r, 0, last)], 0)
        pltpu.make_async_copy(src_ref.at[pl.ds(tok * n, n)], buf_ref.at[pl.ds(r * n, n)], sem).start()
        return carry

    lax.fori_loop(0, tm, start, 0, unroll=8)
    pltpu.make_async_copy(src_ref.at[pl.ds(0, tm * n)], buf_ref, sem).wait()
    o_ref[...] = _load_slabs(buf_ref, 0, tm, n).astype(o_ref.dtype)


def _gather_rows(src_slabs, tok_sorted, block_first, block_valid, tm, d, out_dtype):
    n = d // LANES
    n_blocks = block_first.shape[0]
    return pl.pallas_call(
        _gather_kernel,
        grid_spec=pltpu.PrefetchScalarGridSpec(
            num_scalar_prefetch=3,
            grid=(n_blocks,),
            in_specs=[pl.BlockSpec(memory_space=pl.ANY)],
            out_specs=pl.BlockSpec((tm, d), lambda i, *_: (i, 0)),
            scratch_shapes=[pltpu.VMEM((tm * n, LANES), src_slabs.dtype), pltpu.SemaphoreType.DMA(())],
        ),
        out_shape=jax.ShapeDtypeStruct((n_blocks * tm, d), out_dtype),
        compiler_params=_params("arbitrary"),
    )(tok_sorted, block_first, block_valid, src_slabs)


def _expert_changed(be_ref, i):
    return jnp.logical_or(i == 0, be_ref[i] != be_ref[jnp.maximum(i - 1, 0)])


def _moe_up_kernel(be_ref, used_ref, x_ref, wg_ref, wu_ref, bg_ref, bu_ref, h_ref, wg_bf, wu_bf):
    i = pl.program_id(1)

    @pl.when(_expert_changed(be_ref, i))
    def _():
        wg_bf[...] = wg_ref[0].astype(BF16)
        wu_bf[...] = wu_ref[0].astype(BF16)

    @pl.when(i < used_ref[0])
    def _():
        x = x_ref[...]
        gate = jnp.minimum(_dot(x, wg_bf[...]) + bg_ref[0], SWIGLU_LIMIT)
        up = jnp.clip(_dot(x, wu_bf[...]) + bu_ref[0], -SWIGLU_LIMIT, SWIGLU_LIMIT)
        h_ref[...] = ((up + 1.0) * (gate * _sigmoid(gate * SWIGLU_ALPHA))).astype(h_ref.dtype)

    @pl.when(i >= used_ref[0])
    def _():
        h_ref[...] = jnp.zeros(h_ref.shape, h_ref.dtype)


def _moe_down_kernel(be_ref, used_ref, h_ref, wd_ref, bd_ref, y_ref, wd_bf, slab_ref):
    i = pl.program_id(1)

    @pl.when(_expert_changed(be_ref, i))
    def _():
        wd_bf[...] = wd_ref[0].astype(BF16)

    @pl.when(i < used_ref[0])
    def _():
        _store_slabs(slab_ref, 0, _dot(h_ref[...], wd_bf[...]) + bd_ref[0])
        y_ref[...] = slab_ref[...].reshape(y_ref.shape)

    @pl.when(i >= used_ref[0])
    def _():
        y_ref[...] = jnp.zeros(y_ref.shape, y_ref.dtype)


def _moe_up(xs, block_e, n_used, w_gate, b_gate, w_up, b_up, tm, tn):
    n_pad, d = xs.shape
    ne, _, ff = w_gate.shape
    w_spec = pl.BlockSpec((1, d, tn), lambda j, i, be, nu: (be[i], 0, j))
    b_spec = pl.BlockSpec((1, 1, tn), lambda j, i, be, nu: (be[i], 0, j))
    return pl.pallas_call(
        _moe_up_kernel,
        grid_spec=pltpu.PrefetchScalarGridSpec(
            num_scalar_prefetch=2,
            grid=(ff // tn, n_pad // tm),
            in_specs=[pl.BlockSpec((tm, d), lambda j, i, be, nu: (i, 0)),
                      w_spec, w_spec, b_spec, b_spec],
            out_specs=pl.BlockSpec((tm, tn), lambda j, i, be, nu: (i, j)),
            scratch_shapes=[pltpu.VMEM((d, tn), BF16), pltpu.VMEM((d, tn), BF16)],
        ),
        out_shape=jax.ShapeDtypeStruct((n_pad, ff), BF16),
        compiler_params=_params("arbitrary", "arbitrary"),
    )(block_e, n_used, xs, w_gate, w_up, b_gate.reshape(ne, 1, ff), b_up.reshape(ne, 1, ff))


def _moe_down(hs, block_e, n_used, w_down, b_down, tm, tn):
    n_pad, ff = hs.shape
    ne, _, d = w_down.shape
    nt = tn // LANES
    y = pl.pallas_call(
        _moe_down_kernel,
        grid_spec=pltpu.PrefetchScalarGridSpec(
            num_scalar_prefetch=2,
            grid=(d // tn, n_pad // tm),
            in_specs=[pl.BlockSpec((tm, ff), lambda j, i, be, nu: (i, 0)),
                      pl.BlockSpec((1, ff, tn), lambda j, i, be, nu: (be[i], 0, j)),
                      pl.BlockSpec((1, 1, tn), lambda j, i, be, nu: (be[i], 0, j))],
            out_specs=pl.BlockSpec((tm, nt, LANES), lambda j, i, be, nu: (i, j, 0)),
            scratch_shapes=[pltpu.VMEM((ff, tn), BF16), pltpu.VMEM((tm * nt, LANES), F32)],
        ),
        out_shape=jax.ShapeDtypeStruct((n_pad, d // LANES, LANES), F32),
        compiler_params=_params("arbitrary", "arbitrary"),
    )(block_e, n_used, hs, w_down, b_down.reshape(ne, 1, d))
    return y.reshape(n_pad * (d // LANES), LANES)


def _combine_ple_kernel(pos_ref, y_ref, w_ref, h_ref, p_ref, gffn_ref, gple_ref, wgate_ref,
                        wproj_ref, gpost_ref, o_ref, buf_ref, sem):
    tm, d = h_ref.shape
    n = d // LANES

    def start(r, carry):
        pltpu.make_async_copy(y_ref.at[pl.ds(pos_ref[0, 0, r] * n, n)], buf_ref.at[pl.ds(r * n, n)],
                              sem).start()
        return carry

    lax.fori_loop(0, TOP_K * tm, start, 0, unroll=8)
    pltpu.make_async_copy(y_ref.at[pl.ds(0, TOP_K * tm * n)], buf_ref, sem).wait()
    w = w_ref[...]
    ff = w[:, 0:1] * _load_slabs(buf_ref, 0, tm, n)
    for kk in range(1, TOP_K):
        ff = ff + w[:, kk:kk + 1] * _load_slabs(buf_ref, kk * tm * n, tm, n)
    h = h_ref[...] + _rms(ff, gffn_ref[...])
    gate = _sigmoid(_dot(_rms(h, gple_ref[...]).astype(BF16), wgate_ref[...]))
    emb = _dot(p_ref[...].astype(BF16), wproj_ref[...])
    o_ref[...] = h + _rms(gate * emb, gpost_ref[...])


def _combine_ple(y_rows, pos, top_w, h, p, ffn_gain, ple_gain, w_ple_gate, w_ple_proj,
                 ple_post_gain, tm):
    t, d = h.shape
    rows = lambda width: pl.BlockSpec((tm, width), lambda i: (i, 0))
    whole = lambda a: pl.BlockSpec(a.shape, lambda i: (0, 0), pipeline_mode=pl.Buffered(1))
    consts = (ffn_gain.reshape(1, d), ple_gain.reshape(1, d), w_ple_gate, w_ple_proj,
              ple_post_gain.reshape(1, d))
    return pl.pallas_call(
        _combine_ple_kernel,
        grid=(t // tm,),
        in_specs=[
            pl.BlockSpec((1, 1, TOP_K * tm), lambda i: (i, 0, 0), memory_space=pltpu.SMEM),
            pl.BlockSpec(memory_space=pl.ANY),
            rows(TOP_K), rows(d), rows(p.shape[1]),
        ] + [whole(a) for a in consts],
        out_specs=rows(d),
        out_shape=jax.ShapeDtypeStruct((t, d), F32),
        scratch_shapes=[pltpu.VMEM((TOP_K * tm * (d // LANES), LANES), F32),
                        pltpu.SemaphoreType.DMA(())],
        compiler_params=_params("arbitrary"),
    )(pos, y_rows, top_w, h, p, *consts)


def _routing_tables(top_e, tm, tok_block):
    t = top_e.shape[0]
    n = t * TOP_K
    n_blocks = n // tm + N_EXPERTS
    e_flat = top_e.reshape(n)
    experts = jnp.arange(N_EXPERTS, dtype=jnp.int32)
    order = jnp.argsort(e_flat, stable=True).astype(jnp.int32)
    rank = jnp.argsort(order).astype(jnp.int32)
    onehot = e_flat[:, None] == experts[None, :]
    counts = jnp.sum(onehot, axis=0, dtype=jnp.int32)
    group_start = jnp.cumsum(counts) - counts
    padded = ((counts + tm - 1) // tm) * tm
    padded_end = jnp.cumsum(padded)
    padded_start = padded_end - padded
    shift = padded_start - group_start
    pos = (rank + jnp.sum(jnp.where(onehot, shift[None, :], 0), axis=1)).reshape(t, TOP_K)
    n_used = (padded_end[-1] // tm).astype(jnp.int32).reshape(1)
    offs = jnp.arange(n_blocks, dtype=jnp.int32) * tm
    block_e = jnp.sum(padded_end[None, :] <= offs[:, None], axis=1, dtype=jnp.int32)
    last_e = jnp.sum(padded_end <= (n_used[0] - 1) * tm, dtype=jnp.int32)
    block_e = jnp.minimum(jnp.where(offs < padded_end[-1], block_e, last_e), N_EXPERTS - 1)
    sel = block_e[:, None] == experts[None, :]
    pick = lambda table: jnp.sum(jnp.where(sel, table[None, :], 0), axis=1)
    in_group = offs - pick(padded_start)
    block_first = (pick(group_start) + in_group).astype(jnp.int32)
    block_valid = jnp.where(offs < padded_end[-1], jnp.clip(pick(counts) - in_group, 0, tm), 0)
    pos_blocks = pos.reshape(t // tok_block, tok_block, TOP_K).transpose(0, 2, 1)
    return (order // TOP_K, block_first, block_valid.astype(jnp.int32),
            pos_blocks.reshape(t // tok_block, 1, TOP_K * tok_block), block_e, n_used)


def _layer(h, p, pre_mix_norm, w_in, b_fgate, conv_w, conv_b, dt_bias, a_log, d_skip,
           attn_out_norm, ssm_out_norm, w_out, post_mix_norm, pre_ffn_norm, w_router, b_router,
           w_gate, b_gate, w_up, b_up, w_down, b_down, post_ffn_norm, ple_norm, w_ple_gate,
           w_ple_proj, ple_post_norm):
    bsz, seq, d = h.shape
    t = bsz * seq
    att_w = ATT_HEADS * ATT_HEAD_DIM
    heads = dt_bias.shape[0]
    ssm_w = heads * SSM_HEAD_DIM
    conv_ch = conv_w.shape[1]
    o_fg = 3 * att_w
    o_z = o_fg + ATT_HEADS
    o_xbc = o_z + ssm_w
    o_dt = o_xbc + conv_ch
    x2 = h.reshape(t, d)

    w_qkv = w_in[:, :o_fg].astype(BF16)
    w_z = w_in[:, o_z:o_xbc].astype(BF16)
    w_xbc = w_in[:, o_xbc:o_dt].astype(BF16)
    small_pad = LANES - ATT_HEADS - heads
    w_small = jnp.concatenate([w_in[:, o_fg:o_z], w_in[:, o_dt:], jnp.zeros((d, small_pad), F32)], axis=1)

    tm = min(1024, t)
    qkv = _rms_matmul(x2, pre_mix_norm, w_qkv, BF16, tm, 1024)
    zz = _rms_matmul(x2, pre_mix_norm, w_z, F32, tm, 1024)
    xbc = _rms_matmul(x2, pre_mix_norm, w_xbc, F32, tm, 1024)
    small = _rms_matmul(x2, pre_mix_norm, w_small, F32, tm, LANES).reshape(bsz, seq, LANES)

    fg_t = jnp.swapaxes(small[:, :, :ATT_HEADS], 1, 2)
    dt_raw = small[:, :, ATT_HEADS:ATT_HEADS + heads]
    c_t = _fg_cumsum(fg_t, b_fgate)
    att = _fox_attention(qkv.reshape(bsz, seq, 3 * att_w), c_t, min(512, seq))
    y = _ssd(xbc.reshape(bsz, seq, conv_ch), zz.reshape(bsz, seq, ssm_w), dt_raw,
             jnp.swapaxes(dt_raw, 1, 2), conv_w, conv_b, dt_bias, a_log, d_skip, ssm_out_norm)

    w_out_bf = w_out.astype(BF16)
    h1, hn, logits = _out_proj(att.reshape(t, att_w), y.reshape(t, ssm_w), x2, attn_out_norm,
                               w_out_bf[:att_w], w_out_bf[att_w:], post_mix_norm, pre_ffn_norm,
                               w_router, b_router, min(256, t))
    top_e, top_w = _router_topk(logits, min(2048, t))

    tok_block = min(256, t)
    tok_sorted, block_first, block_valid, pos, block_e, n_used = _routing_tables(
        top_e, MOE_ROWS, tok_block)
    xs = _gather_rows(hn, tok_sorted, block_first, block_valid, MOE_ROWS, d, BF16)
    hs = _moe_up(xs, block_e, n_used, w_gate, b_gate, w_up, b_up, MOE_ROWS, min(512, w_gate.shape[2]))
    y_rows = _moe_down(hs, block_e, n_used, w_down, b_down, MOE_ROWS, min(1024, d))
    out = _combine_ple(y_rows, pos, top_w, h1, p.reshape(t, -1), post_ffn_norm, ple_norm,
                       w_ple_gate.astype(BF16), w_ple_proj.astype(BF16), ple_post_norm, tok_block)
    return out.reshape(bsz, seq, d)


def kernel(x, p, pre_mix_norm, w_in, b_fgate, conv_w, conv_b, dt_bias, a_log, d_skip, attn_out_norm, ssm_out_norm, w_out, post_mix_norm, pre_ffn_norm, w_router, b_router, w_gate, b_gate, w_up, b_up, w_down, b_down, post_ffn_norm, ple_norm, w_ple_gate, w_ple_proj, ple_post_norm):
    h = x
    for i in range(p.shape[0]):
        h = _layer(h, p[i], pre_mix_norm[i], w_in[i], b_fgate[i], conv_w[i], conv_b[i], dt_bias[i],
                   a_log[i], d_skip[i], attn_out_norm[i], ssm_out_norm[i], w_out[i],
                   post_mix_norm[i], pre_ffn_norm[i], w_router[i], b_router[i], w_gate[i], b_gate[i],
                   w_up[i], b_up[i], w_down[i], b_down[i], post_ffn_norm[i], ple_norm[i],
                   w_ple_gate[i], w_ple_proj[i], ple_post_norm[i])
    return h
```

```python
import functools

import jax
import jax.numpy as jnp
from jax import lax
from jax.experimental import pallas as pl
from jax.experimental.pallas import tpu as pltpu

F32 = jnp.float32
BF16 = jnp.bfloat16

ATT_HEADS = 8
ATT_HEAD_DIM = 128
SSM_HEAD_DIM = 64
SSM_GROUPS = 8
SSM_STATE = 128
SSM_CHUNK = 128
CONV_WIDTH = 4
N_EXPERTS = 32
TOP_K = 4
SWIGLU_LIMIT = 7.0
SWIGLU_ALPHA = 1.702
EPS = 1e-6
LOG2E = 1.4426950408889634

LANES = 128
SUBLANES = 8
VMEM_LIMIT_BYTES = 56 * 1024 * 1024

MOE_ROWS = 512


def _params(*semantics):
    return pltpu.CompilerParams(dimension_semantics=semantics, vmem_limit_bytes=VMEM_LIMIT_BYTES)


def _rms(x, gain):
    return x * lax.rsqrt(jnp.mean(x * x, axis=-1, keepdims=True) + EPS) * gain


def _sigmoid(x):
    return 1.0 / (1.0 + jnp.exp(-x))


def _softplus(x):
    return jnp.maximum(x, 0.0) + jnp.log1p(jnp.exp(-jnp.abs(x)))


def _split3(x):
    x1 = x.astype(BF16)
    r1 = x - x1.astype(F32)
    x2 = r1.astype(BF16)
    x3 = (r1 - x2.astype(F32)).astype(BF16)
    return x1, x2, x3


def _dot(a, b):
    return jnp.dot(a, b, preferred_element_type=F32)


def _dot_exact_rhs(x, m):
    x1, x2, x3 = _split3(x)
    return _dot(x1, m) + _dot(x2, m) + _dot(x3, m)


def _dot_exact_lhs(m, x):
    x1, x2, x3 = _split3(x)
    return _dot(m, x1) + _dot(m, x2) + _dot(m, x3)


def _store_slabs(ref, base, value):
    rows, width = value.shape
    n = width // LANES
    for c in range(n):
        ref[pl.ds(base + c, rows, stride=n), :] = value[:, c * LANES:(c + 1) * LANES]


def _load_slabs(ref, base, rows, n):
    return jnp.concatenate([ref[pl.ds(base + c, rows, stride=n), :] for c in range(n)], axis=1)


def _rms_matmul_kernel(x_ref, g_ref, w_ref, o_ref, xn_ref):
    @pl.when(pl.program_id(1) == 0)
    def _():
        xn_ref[...] = _rms(x_ref[...], g_ref[...]).astype(xn_ref.dtype)

    if xn_ref.dtype == F32:
        acc = jnp.dot(xn_ref[...], w_ref[...], preferred_element_type=F32,
                      precision=lax.Precision.HIGHEST)
    else:
        acc = _dot(xn_ref[...], w_ref[...])
    o_ref[...] = acc.astype(o_ref.dtype)


def _rms_matmul(x, gain, w, out_dtype, tm, tn):
    m, k = x.shape
    n = w.shape[1]
    return pl.pallas_call(
        _rms_matmul_kernel,
        grid=(m // tm, n // tn),
        in_specs=[
            pl.BlockSpec((tm, k), lambda i, j: (i, 0)),
            pl.BlockSpec((1, k), lambda i, j: (0, 0)),
            pl.BlockSpec((k, tn), lambda i, j: (0, j)),
        ],
        out_specs=pl.BlockSpec((tm, tn), lambda i, j: (i, j)),
        out_shape=jax.ShapeDtypeStruct((m, n), out_dtype),
        scratch_shapes=[pltpu.VMEM((tm, k), w.dtype)],
        compiler_params=_params("parallel", "arbitrary"),
    )(x, gain.reshape(1, k), w)


def _fg_cumsum_kernel(fg_ref, b_ref, c_ref):
    seq = fg_ref.shape[2]
    row = lax.broadcasted_iota(jnp.int32, (LANES, LANES), 0)
    col = lax.broadcasted_iota(jnp.int32, (LANES, LANES), 1)
    triu = (row <= col).astype(BF16)
    carry = jnp.zeros((fg_ref.shape[1], 1), F32)
    for ci in range(seq // LANES):
        z = fg_ref[0, :, ci * LANES:(ci + 1) * LANES] + b_ref[...]
        ls = jnp.minimum(z, 0.0) - jnp.log1p(jnp.exp(-jnp.abs(z)))
        cs = _dot_exact_rhs(ls, triu) + carry
        c_ref[0, :, ci * LANES:(ci + 1) * LANES] = cs * LOG2E
        carry = cs[:, LANES - 1:LANES]


def _fg_cumsum(fg_t, b_fgate):
    bsz, heads, seq = fg_t.shape
    return pl.pallas_call(
        _fg_cumsum_kernel,
        grid=(bsz,),
        in_specs=[
            pl.BlockSpec((1, heads, seq), lambda b: (b, 0, 0)),
            pl.BlockSpec((heads, 1), lambda b: (0, 0)),
        ],
        out_specs=pl.BlockSpec((1, heads, seq), lambda b: (b, 0, 0)),
        out_shape=jax.ShapeDtypeStruct((bsz, heads, seq), F32),
        compiler_params=_params("parallel"),
    )(fg_t, b_fgate.reshape(heads, 1))


def _attn_kernel(q_ref, k_ref, v_ref, ck_ref, o_ref, *, blk):
    qi = pl.program_id(2)
    q = q_ref[0]

    def step(j, carry, masked):
        m, l, acc = carry
        start = pl.multiple_of(j * blk, blk)
        kj = k_ref[0, pl.ds(start, blk), :]
        vj = v_ref[0, pl.ds(start, blk), :]
        s = lax.dot_general(q, kj, (((1,), (1,)), ((), ())), preferred_element_type=F32)
        s = s - ck_ref[0, 0, j]
        if masked:
            row = lax.broadcasted_iota(jnp.int32, (blk, blk), 0)
            col = lax.broadcasted_iota(jnp.int32, (blk, blk), 1)
            s = jnp.where(col <= row, s, -jnp.inf)
        m_new = jnp.maximum(m, jnp.max(s, axis=-1, keepdims=True))
        alpha = jnp.exp2(m - m_new)
        p = jnp.exp2(s - m_new)
        l = alpha * l + jnp.sum(p, axis=-1, keepdims=True)
        acc = alpha * acc + _dot(p.astype(BF16), vj)
        return m_new, l, acc

    init = (jnp.full((blk, 1), -jnp.inf, F32), jnp.zeros((blk, 1), F32),
            jnp.zeros((blk, q.shape[-1]), F32))
    carry = lax.fori_loop(0, qi, lambda j, c: step(j, c, False), init)
    _, l, acc = step(qi, carry, True)
    o_ref[0] = (acc / l).astype(o_ref.dtype)


def _fox_attention(qkv, c_t, blk):
    bsz, seq, _ = qkv.shape
    h, dh = ATT_HEADS, ATT_HEAD_DIM
    ck = c_t.reshape(bsz, h, seq // blk, 1, blk)
    return pl.pallas_call(
        functools.partial(_attn_kernel, blk=blk),
        grid=(bsz, h, seq // blk),
        in_specs=[
            pl.BlockSpec((1, blk, dh), lambda b, hh, i: (b, i, hh)),
            pl.BlockSpec((1, seq, dh), lambda b, hh, i: (b, 0, h + hh)),
            pl.BlockSpec((1, seq, dh), lambda b, hh, i: (b, 0, 2 * h + hh)),
            pl.BlockSpec((1, 1, seq // blk, 1, blk), lambda b, hh, i: (b, hh, 0, 0, 0)),
        ],
        out_specs=pl.BlockSpec((1, blk, dh), lambda b, hh, i: (b, i, hh)),
        out_shape=jax.ShapeDtypeStruct((bsz, seq, h * dh), BF16),
        compiler_params=_params("parallel", "parallel", "arbitrary"),
    )(qkv, qkv, qkv, ck)


def _ssd_kernel(xbc_ref, z_ref, dt_ref, dtt_ref, cw_ref, cb_ref, dtb_ref, dtbt_ref, alog_ref,
                alogt_ref, dskip_ref, gain_ref, expand_ref, y_ref, ext_ref, act_ref, state_ref):
    chunk = SSM_CHUNK
    width = y_ref.shape[2]
    heads = dt_ref.shape[2]
    per_group = heads // SSM_GROUPS
    gw = per_group * SSM_HEAD_DIM
    nstate = SSM_STATE
    halo = SUBLANES
    c_idx = pl.program_id(1)

    @pl.when(c_idx == 0)
    def _():
        ext_ref[0:halo, :] = jnp.zeros((halo, ext_ref.shape[1]), F32)
        state_ref[...] = jnp.zeros(state_ref.shape, F32)

    ext_ref[halo:halo + chunk, :] = xbc_ref[0]
    conv_ch = ext_ref.shape[1]
    col_tile = 4 * LANES
    for ct in range(conv_ch // col_tile):
        sl = slice(ct * col_tile, (ct + 1) * col_tile)
        acc = cb_ref[:, sl] + cw_ref[CONV_WIDTH - 1:CONV_WIDTH, sl] * ext_ref[halo:halo + chunk, sl]
        for kk in range(CONV_WIDTH - 1):
            off = halo - (CONV_WIDTH - 1) + kk
            acc = acc + cw_ref[kk:kk + 1, sl] * ext_ref[off:off + chunk, sl]
        act_ref[:, sl] = acc * _sigmoid(acc)
    ext_ref[0:halo, :] = xbc_ref[0, chunk - halo:chunk, :]

    row = lax.broadcasted_iota(jnp.int32, (chunk, chunk), 0)
    col = lax.broadcasted_iota(jnp.int32, (chunk, chunk), 1)
    causal = row >= col
    tril = causal.astype(BF16)
    triu = (row <= col).astype(BF16)
    dt = _softplus(dt_ref[0] + dtb_ref[...])
    adt = dt * (-jnp.exp(alog_ref[...]))
    acs = _dot_exact_lhs(tril, adt)
    dtt = _softplus(dtt_ref[0] + dtbt_ref[...])
    acs_t = _dot_exact_rhs(dtt * (-jnp.exp(alogt_ref[...])), triu)

    expand = expand_ref[...]
    dt_x = _dot_exact_rhs(dt, expand)
    acs_x = _dot_exact_rhs(acs, expand)
    last_x = acs_x[chunk - 1:chunk, :]
    xs = act_ref[:, 0:width]
    xc = xs * dt_x
    xc_bf = xc.astype(BF16)
    xcd_bf = (xc * jnp.exp(last_x - acs_x)).astype(BF16)
    in_decay = jnp.exp(acs_x)
    chunk_decay = jnp.exp(last_x)
    lane = lax.broadcasted_iota(jnp.int32, (chunk, LANES), 1)
    low_half = lane < SSM_HEAD_DIM

    for g in range(SSM_GROUPS):
        gsl = slice(g * gw, (g + 1) * gw)
        b_g = act_ref[:, width + g * nstate:width + (g + 1) * nstate]
        c_g = act_ref[:, width + (SSM_GROUPS + g) * nstate:width + (SSM_GROUPS + g + 1) * nstate]
        c_bf = c_g.astype(BF16)
        cb = lax.dot_general(c_bf, b_g.astype(BF16), (((1,), (1,)), ((), ())),
                             preferred_element_type=F32)
        pieces = []
        for pair in range(per_group // 2):
            ms = []
            for r in (2 * pair, 2 * pair + 1):
                hd = g * per_group + r
                seg = acs[:, hd:hd + 1] - acs_t[hd:hd + 1, :]
                decay = jnp.exp(jnp.where(causal, seg, -jnp.inf))
                ms.append((cb * decay).astype(BF16))
            xpair = xc_bf[:, g * gw + pair * LANES:g * gw + (pair + 1) * LANES]
            zero = jnp.zeros_like(xpair)
            rhs = jnp.concatenate([jnp.where(low_half, xpair, zero),
                                   jnp.where(low_half, zero, xpair)], axis=0)
            pieces.append(_dot(jnp.concatenate(ms, axis=1), rhs))
        y_diag = jnp.concatenate(pieces, axis=1)
        state = state_ref[g]
        y_off = _dot(c_bf, state.astype(BF16)) * in_decay[:, gsl]
        y = y_diag + y_off + dskip_ref[:, gsl] * xs[:, gsl]
        zg = z_ref[0, :, gsl]
        y = y * (zg * _sigmoid(zg))
        y = y * lax.rsqrt(jnp.mean(y * y, axis=-1, keepdims=True) + EPS) * gain_ref[:, gsl]
        y_ref[0, :, gsl] = y.astype(y_ref.dtype)
        state_ref[g] = state * chunk_decay[:, gsl] + _dot(b_g.T.astype(BF16), xcd_bf[:, gsl])


def _ssd(xbc, z, dt_raw, dt_raw_t, conv_w, conv_b, dt_bias, a_log, d_skip, gain):
    bsz, seq, conv_ch = xbc.shape
    width = z.shape[2]
    heads = dt_raw.shape[2]
    chunk = SSM_CHUNK
    gw = width // SSM_GROUPS
    expand = (jnp.arange(width)[None, :] // SSM_HEAD_DIM == jnp.arange(heads)[:, None]).astype(BF16)
    dskip_x = jnp.repeat(d_skip, SSM_HEAD_DIM).reshape(1, width)
    const = lambda shape: pl.BlockSpec(shape, lambda b, c: (0,) * len(shape))
    return pl.pallas_call(
        _ssd_kernel,
        grid=(bsz, seq // chunk),
        in_specs=[
            pl.BlockSpec((1, chunk, conv_ch), lambda b, c: (b, c, 0)),
            pl.BlockSpec((1, chunk, width), lambda b, c: (b, c, 0)),
            pl.BlockSpec((1, chunk, heads), lambda b, c: (b, c, 0)),
            pl.BlockSpec((1, heads, chunk), lambda b, c: (b, 0, c)),
            const((CONV_WIDTH, conv_ch)),
            const((1, conv_ch)),
            const((1, heads)),
            const((heads, 1)),
            const((1, heads)),
            const((heads, 1)),
            const((1, width)),
            const((1, width)),
            const((heads, width)),
        ],
        out_specs=pl.BlockSpec((1, chunk, width), lambda b, c: (b, c, 0)),
        out_shape=jax.ShapeDtypeStruct((bsz, seq, width), BF16),
        scratch_shapes=[
            pltpu.VMEM((SUBLANES + chunk, conv_ch), F32),
            pltpu.VMEM((chunk, conv_ch), F32),
            pltpu.VMEM((SSM_GROUPS, SSM_STATE, gw), F32),
        ],
        compiler_params=_params("parallel", "arbitrary"),
    )(xbc, z, dt_raw, dt_raw_t, conv_w, conv_b.reshape(1, conv_ch), dt_bias.reshape(1, heads),
      dt_bias.reshape(heads, 1), a_log.reshape(1, heads), a_log.reshape(heads, 1), dskip_x,
      gain.reshape(1, width), expand)


def _out_proj_kernel(att_ref, y_ref, x_ref, ga_ref, wa_ref, wy_ref, gpost_ref, gpre_ref, wr_ref,
                     br_ref, h_ref, hn_ref, logit_ref):
    att_n = _rms(att_ref[...].astype(F32), ga_ref[...]).astype(BF16)
    mixed = _dot(att_n, wa_ref[...]) + _dot(y_ref[...], wy_ref[...])
    h = x_ref[...] + _rms(mixed, gpost_ref[...])
    h_ref[...] = h
    hn = _rms(h, gpre_ref[...])
    _store_slabs(hn_ref, 0, hn)
    logit_ref[...] = jnp.dot(hn, wr_ref[...], preferred_element_type=F32,
                             precision=lax.Precision.HIGHEST) + br_ref[...]


def _out_proj(att, y, x, attn_gain, w_att, w_y, post_gain, pre_ffn_gain, w_router, b_router, tm):
    t, d = x.shape
    ne = w_router.shape[1]
    rows = lambda width: pl.BlockSpec((tm, width), lambda i: (i, 0))
    whole = lambda a: pl.BlockSpec(a.shape, lambda i: (0, 0), pipeline_mode=pl.Buffered(1))
    args = (att, y, x, attn_gain.reshape(1, -1), w_att, w_y, post_gain.reshape(1, d),
            pre_ffn_gain.reshape(1, d), w_router, b_router.reshape(1, ne))
    return pl.pallas_call(
        _out_proj_kernel,
        grid=(t // tm,),
        in_specs=[rows(att.shape[1]), rows(y.shape[1]), rows(d)] + [whole(a) for a in args[3:]],
        out_specs=[rows(d), pl.BlockSpec((tm * (d // LANES), LANES), lambda i: (i, 0)), rows(ne)],
        out_shape=[jax.ShapeDtypeStruct((t, d), F32),
                   jax.ShapeDtypeStruct((t * (d // LANES), LANES), F32),
                   jax.ShapeDtypeStruct((t, ne), F32)],
        compiler_params=_params("parallel"),
    )(*args)


def _topk_kernel(logit_ref, e_ref, w_ref):
    l = logit_ref[...]
    ne = l.shape[1]
    lane = lax.broadcasted_iota(jnp.int32, l.shape, 1)
    slot = lax.broadcasted_iota(jnp.int32, e_ref.shape, 1)
    e_out = jnp.zeros(e_ref.shape, jnp.int32)
    p_out = jnp.zeros(w_ref.shape, F32)
    top = None
    for kk in range(TOP_K):
        m = jnp.max(l, axis=-1, keepdims=True)
        idx = jnp.min(jnp.where(l == m, lane, ne), axis=-1, keepdims=True)
        if top is None:
            top = m
        e_out = jnp.where(slot == kk, idx, e_out)
        p_out = jnp.where(slot == kk, jnp.exp(m - top), p_out)
        l = jnp.where(lane == idx, -jnp.inf, l)
    e_ref[...] = e_out
    w_ref[...] = p_out / jnp.sum(p_out, axis=-1, keepdims=True)


def _router_topk(logits, tm):
    t, ne = logits.shape
    return pl.pallas_call(
        _topk_kernel,
        grid=(t // tm,),
        in_specs=[pl.BlockSpec((tm, ne), lambda i: (i, 0))],
        out_specs=[pl.BlockSpec((tm, TOP_K), lambda i: (i, 0))] * 2,
        out_shape=[jax.ShapeDtypeStruct((t, TOP_K), jnp.int32),
                   jax.ShapeDtypeStruct((t, TOP_K), F32)],
        compiler_params=_params("parallel"),
    )(logits)


DMA_ISSUE_UNROLL = 8


def _issue_row_copies(count, row_index, src_ref, dst_ref, sem, n):
    def body(g, carry):
        for u in range(DMA_ISSUE_UNROLL):
            r = g * DMA_ISSUE_UNROLL + u
            pltpu.make_async_copy(src_ref.at[pl.ds(row_index(r) * n, n)],
                                  dst_ref.at[pl.ds(r * n, n)], sem).start(priority=u % 2)
        return carry

    lax.fori_loop(0, count // DMA_ISSUE_UNROLL, body, 0)


def _gather_kernel(tok_ref, first_ref, valid_ref, used_ref, src_ref, o_ref, buf_ref, sem):
    tm = o_ref.shape[0]
    n = buf_ref.shape[1] // tm
    i = pl.program_id(0)
    used = used_ref[0]
    last = tok_ref.shape[0] - 1

    def issue(block, slot):
        first = first_ref[block]
        valid = valid_ref[block]
        token = lambda r: jnp.where(r < valid, tok_ref[jnp.clip(first + r, 0, last)], 0)
        _issue_row_copies(tm, token, src_ref, buf_ref.at[slot], sem.at[slot], n)

    for slot in range(2):
        @pl.when(jnp.logical_and(i % 2 == slot, i < used))
        def _(slot=slot):
            if slot == 0:
                @pl.when(i == 0)
                def _():
                    issue(0, 0)

            @pl.when(i + 1 < used)
            def _():
                issue(i + 1, 1 - slot)

            pltpu.make_async_copy(src_ref.at[pl.ds(0, tm * n)], buf_ref.at[slot], sem.at[slot]).wait()
            o_ref[...] = _load_slabs(buf_ref.at[slot], 0, tm, n).astype(o_ref.dtype)

    @pl.when(i >= used)
    def _():
        o_ref[...] = jnp.zeros(o_ref.shape, o_ref.dtype)


def _gather_rows(src_slabs, tok_sorted, block_first, block_valid, n_used, tm, d, out_dtype):
    n = d // LANES
    n_blocks = block_first.shape[0]
    return pl.pallas_call(
        _gather_kernel,
        grid_spec=pltpu.PrefetchScalarGridSpec(
            num_scalar_prefetch=4,
            grid=(n_blocks,),
            in_specs=[pl.BlockSpec(memory_space=pl.ANY)],
            out_specs=pl.BlockSpec((tm, d), lambda i, *_: (i, 0)),
            scratch_shapes=[pltpu.VMEM((2, tm * n, LANES), src_slabs.dtype),
                            pltpu.SemaphoreType.DMA((2,))],
        ),
        out_shape=jax.ShapeDtypeStruct((n_blocks * tm, d), out_dtype),
        compiler_params=_params("arbitrary"),
    )(tok_sorted, block_first, block_valid, n_used, src_slabs)


def _expert_changed(be_ref, i):
    return jnp.logical_or(i == 0, be_ref[i] != be_ref[jnp.maximum(i - 1, 0)])


def _moe_up_kernel(be_ref, used_ref, x_ref, wg_ref, wu_ref, bg_ref, bu_ref, h_ref, wg_bf, wu_bf):
    i = pl.program_id(1)

    @pl.when(_expert_changed(be_ref, i))
    def _():
        wg_bf[...] = wg_ref[0].astype(BF16)
        wu_bf[...] = wu_ref[0].astype(BF16)

    @pl.when(i < used_ref[0])
    def _():
        x = x_ref[...]
        gate = jnp.minimum(_dot(x, wg_bf[...]) + bg_ref[0], SWIGLU_LIMIT)
        up = jnp.clip(_dot(x, wu_bf[...]) + bu_ref[0], -SWIGLU_LIMIT, SWIGLU_LIMIT)
        h_ref[...] = ((up + 1.0) * (gate * _sigmoid(gate * SWIGLU_ALPHA))).astype(h_ref.dtype)

    @pl.when(i >= used_ref[0])
    def _():
        h_ref[...] = jnp.zeros(h_ref.shape, h_ref.dtype)


def _moe_down_kernel(be_ref, used_ref, h_ref, wd_ref, bd_ref, y_ref, wd_bf, slab_ref):
    i = pl.program_id(1)

    @pl.when(_expert_changed(be_ref, i))
    def _():
        wd_bf[...] = wd_ref[0].astype(BF16)

    @pl.when(i < used_ref[0])
    def _():
        _store_slabs(slab_ref, 0, _dot(h_ref[...], wd_bf[...]) + bd_ref[0])
        y_ref[...] = slab_ref[...].reshape(y_ref.shape)

    @pl.when(i >= used_ref[0])
    def _():
        y_ref[...] = jnp.zeros(y_ref.shape, y_ref.dtype)


def _moe_up(xs, block_e, n_used, w_gate, b_gate, w_up, b_up, tm, tn):
    n_pad, d = xs.shape
    ne, _, ff = w_gate.shape
    w_spec = pl.BlockSpec((1, d, tn), lambda j, i, be, nu: (be[i], 0, j))
    b_spec = pl.BlockSpec((1, 1, tn), lambda j, i, be, nu: (be[i], 0, j))
    return pl.pallas_call(
        _moe_up_kernel,
        grid_spec=pltpu.PrefetchScalarGridSpec(
            num_scalar_prefetch=2,
            grid=(ff // tn, n_pad // tm),
            in_specs=[pl.BlockSpec((tm, d), lambda j, i, be, nu: (i, 0)),
                      w_spec, w_spec, b_spec, b_spec],
            out_specs=pl.BlockSpec((tm, tn), lambda j, i, be, nu: (i, j)),
            scratch_shapes=[pltpu.VMEM((d, tn), BF16), pltpu.VMEM((d, tn), BF16)],
        ),
        out_shape=jax.ShapeDtypeStruct((n_pad, ff), BF16),
        compiler_params=_params("arbitrary", "arbitrary"),
    )(block_e, n_used, xs, w_gate, w_up, b_gate.reshape(ne, 1, ff), b_up.reshape(ne, 1, ff))


def _moe_down(hs, block_e, n_used, w_down, b_down, tm, tn):
    n_pad, ff = hs.shape
    ne, _, d = w_down.shape
    nt = tn // LANES
    y = pl.pallas_call(
        _moe_down_kernel,
        grid_spec=pltpu.PrefetchScalarGridSpec(
            num_scalar_prefetch=2,
            grid=(d // tn, n_pad // tm),
            in_specs=[pl.BlockSpec((tm, ff), lambda j, i, be, nu: (i, 0)),
                      pl.BlockSpec((1, ff, tn), lambda j, i, be, nu: (be[i], 0, j)),
                      pl.BlockSpec((1, 1, tn), lambda j, i, be, nu: (be[i], 0, j))],
            out_specs=pl.BlockSpec((tm, nt, LANES), lambda j, i, be, nu: (i, j, 0)),
            scratch_shapes=[pltpu.VMEM((ff, tn), BF16), pltpu.VMEM((tm * nt, LANES), F32)],
        ),
        out_shape=jax.ShapeDtypeStruct((n_pad, d // LANES, LANES), F32),
        compiler_params=_params("arbitrary", "arbitrary"),
    )(block_e, n_used, hs, w_down, b_down.reshape(ne, 1, d))
    return y.reshape(n_pad * (d // LANES), LANES)


def _combine_ple_kernel(pos_ref, pos_next_ref, y_ref, w_ref, h_ref, p_ref, gffn_ref, gple_ref,
                        wgate_ref, wproj_ref, gpost_ref, o_ref, buf_ref, ff_ref, sem):
    tm, d = h_ref.shape
    n = d // LANES
    i = pl.program_id(0)

    def issue(idx_ref, slot):
        _issue_row_copies(TOP_K * tm, lambda r: idx_ref[0, 0, r], y_ref, buf_ref.at[slot],
                          sem.at[slot], n)

    for slot in range(2):
        @pl.when(i % 2 == slot)
        def _(slot=slot):
            if slot == 0:
                @pl.when(i == 0)
                def _():
                    issue(pos_ref, 0)

            @pl.when(i + 1 < pl.num_programs(0))
            def _():
                issue(pos_next_ref, 1 - slot)

            pltpu.make_async_copy(y_ref.at[pl.ds(0, TOP_K * tm * n)], buf_ref.at[slot],
                                  sem.at[slot]).wait()
            w = w_ref[...]
            ff = w[:, 0:1] * _load_slabs(buf_ref.at[slot], 0, tm, n)
            for kk in range(1, TOP_K):
                ff = ff + w[:, kk:kk + 1] * _load_slabs(buf_ref.at[slot], kk * tm * n, tm, n)
            ff_ref[...] = ff

    h = h_ref[...] + _rms(ff_ref[...], gffn_ref[...])
    gate = _sigmoid(_dot(_rms(h, gple_ref[...]).astype(BF16), wgate_ref[...]))
    emb = _dot(p_ref[...].astype(BF16), wproj_ref[...])
    o_ref[...] = h + _rms(gate * emb, gpost_ref[...])


def _combine_ple(y_rows, pos, top_w, h, p, ffn_gain, ple_gain, w_ple_gate, w_ple_proj,
                 ple_post_gain, tm):
    t, d = h.shape
    rows = lambda width: pl.BlockSpec((tm, width), lambda i: (i, 0))
    whole = lambda a: pl.BlockSpec(a.shape, lambda i: (0, 0), pipeline_mode=pl.Buffered(1))
    consts = (ffn_gain.reshape(1, d), ple_gain.reshape(1, d), w_ple_gate, w_ple_proj,
              ple_post_gain.reshape(1, d))
    steps = t // tm
    return pl.pallas_call(
        _combine_ple_kernel,
        grid=(steps,),
        in_specs=[
            pl.BlockSpec((1, 1, TOP_K * tm), lambda i: (i, 0, 0), memory_space=pltpu.SMEM),
            pl.BlockSpec((1, 1, TOP_K * tm), lambda i: (jnp.minimum(i + 1, steps - 1), 0, 0),
                         memory_space=pltpu.SMEM),
            pl.BlockSpec(memory_space=pl.ANY),
            rows(TOP_K), rows(d), rows(p.shape[1]),
        ] + [whole(a) for a in consts],
        out_specs=rows(d),
        out_shape=jax.ShapeDtypeStruct((t, d), F32),
        scratch_shapes=[pltpu.VMEM((2, TOP_K * tm * (d // LANES), LANES), F32),
                        pltpu.VMEM((tm, d), F32), pltpu.SemaphoreType.DMA((2,))],
        compiler_params=_params("arbitrary"),
    )(pos, pos, y_rows, top_w, h, p, *consts)


def _routing_tables(top_e, tm, tok_block):
    t = top_e.shape[0]
    n = t * TOP_K
    n_blocks = n // tm + N_EXPERTS
    e_flat = top_e.reshape(n)
    experts = jnp.arange(N_EXPERTS, dtype=jnp.int32)
    order = jnp.argsort(e_flat, stable=True).astype(jnp.int32)
    rank = jnp.argsort(order).astype(jnp.int32)
    onehot = e_flat[:, None] == experts[None, :]
    counts = jnp.sum(onehot, axis=0, dtype=jnp.int32)
    group_start = jnp.cumsum(counts) - counts
    padded = ((counts + tm - 1) // tm) * tm
    padded_end = jnp.cumsum(padded)
    padded_start = padded_end - padded
    shift = padded_start - group_start
    pos = (rank + jnp.sum(jnp.where(onehot, shift[None, :], 0), axis=1)).reshape(t, TOP_K)
    n_used = (padded_end[-1] // tm).astype(jnp.int32).reshape(1)
    offs = jnp.arange(n_blocks, dtype=jnp.int32) * tm
    block_e = jnp.sum(padded_end[None, :] <= offs[:, None], axis=1, dtype=jnp.int32)
    last_e = jnp.sum(padded_end <= (n_used[0] - 1) * tm, dtype=jnp.int32)
    block_e = jnp.minimum(jnp.where(offs < padded_end[-1], block_e, last_e), N_EXPERTS - 1)
    sel = block_e[:, None] == experts[None, :]
    pick = lambda table: jnp.sum(jnp.where(sel, table[None, :], 0), axis=1)
    in_group = offs - pick(padded_start)
    block_first = (pick(group_start) + in_group).astype(jnp.int32)
    block_valid = jnp.where(offs < padded_end[-1], jnp.clip(pick(counts) - in_group, 0, tm), 0)
    pos_blocks = pos.reshape(t // tok_block, tok_block, TOP_K).transpose(0, 2, 1)
    return (order // TOP_K, block_first, block_valid.astype(jnp.int32),
            pos_blocks.reshape(t // tok_block, 1, TOP_K * tok_block), block_e, n_used)


def _layer(h, p, pre_mix_norm, w_in, b_fgate, conv_w, conv_b, dt_bias, a_log, d_skip,
           attn_out_norm, ssm_out_norm, w_out, post_mix_norm, pre_ffn_norm, w_router, b_router,
           w_gate, b_gate, w_up, b_up, w_down, b_down, post_ffn_norm, ple_norm, w_ple_gate,
           w_ple_proj, ple_post_norm):
    bsz, seq, d = h.shape
    t = bsz * seq
    att_w = ATT_HEADS * ATT_HEAD_DIM
    heads = dt_bias.shape[0]
    ssm_w = heads * SSM_HEAD_DIM
    conv_ch = conv_w.shape[1]
    o_fg = 3 * att_w
    o_z = o_fg + ATT_HEADS
    o_xbc = o_z + ssm_w
    o_dt = o_xbc + conv_ch
    x2 = h.reshape(t, d)

    q_scale = LOG2E * ATT_HEAD_DIM ** -0.5
    w_qkv = jnp.concatenate([w_in[:, :att_w] * q_scale, w_in[:, att_w:o_fg]], axis=1).astype(BF16)
    w_z = w_in[:, o_z:o_xbc].astype(BF16)
    w_xbc = w_in[:, o_xbc:o_dt].astype(BF16)
    small_pad = LANES - ATT_HEADS - heads
    w_small = jnp.concatenate([w_in[:, o_fg:o_z], w_in[:, o_dt:], jnp.zeros((d, small_pad), F32)], axis=1)

    tm = min(1024, t)
    qkv = _rms_matmul(x2, pre_mix_norm, w_qkv, BF16, tm, 1024)
    zz = _rms_matmul(x2, pre_mix_norm, w_z, F32, tm, 1024)
    xbc = _rms_matmul(x2, pre_mix_norm, w_xbc, F32, tm, 1024)
    small = _rms_matmul(x2, pre_mix_norm, w_small, F32, tm, LANES).reshape(bsz, seq, LANES)

    fg_t = jnp.swapaxes(small[:, :, :ATT_HEADS], 1, 2)
    dt_raw = small[:, :, ATT_HEADS:ATT_HEADS + heads]
    c_t = _fg_cumsum(fg_t, b_fgate)
    att = _fox_attention(qkv.reshape(bsz, seq, 3 * att_w), c_t, min(512, seq))
    y = _ssd(xbc.reshape(bsz, seq, conv_ch), zz.reshape(bsz, seq, ssm_w), dt_raw,
             jnp.swapaxes(dt_raw, 1, 2), conv_w, conv_b, dt_bias, a_log, d_skip, ssm_out_norm)

    w_out_bf = w_out.astype(BF16)
    h1, hn, logits = _out_proj(att.reshape(t, att_w), y.reshape(t, ssm_w), x2, attn_out_norm,
                               w_out_bf[:att_w], w_out_bf[att_w:], post_mix_norm, pre_ffn_norm,
                               w_router, b_router, min(256, t))
    top_e, top_w = _router_topk(logits, min(2048, t))

    tok_block = min(256, t)
    tok_sorted, block_first, block_valid, pos, block_e, n_used = _routing_tables(
        top_e, MOE_ROWS, tok_block)
    xs = _gather_rows(hn, tok_sorted, block_first, block_valid, n_used, MOE_ROWS, d, BF16)
    hs = _moe_up(xs, block_e, n_used, w_gate, b_gate, w_up, b_up, MOE_ROWS, min(1024, w_gate.shape[2]))
    y_rows = _moe_down(hs, block_e, n_used, w_down, b_down, MOE_ROWS, min(1024, d))
    out = _combine_ple(y_rows, pos, top_w, h1, p.reshape(t, -1), post_ffn_norm, ple_norm,
                       w_ple_gate.astype(BF16), w_ple_proj.astype(BF16), ple_post_norm, tok_block)
    return out.reshape(bsz, seq, d)


def kernel(x, p, pre_mix_norm, w_in, b_fgate, conv_w, conv_b, dt_bias, a_log, d_skip, attn_out_norm, ssm_out_norm, w_out, post_mix_norm, pre_ffn_norm, w_router, b_router, w_gate, b_gate, w_up, b_up, w_down, b_down, post_ffn_norm, ple_norm, w_ple_gate, w_ple_proj, ple_post_norm):
    h = x
    for i in range(p.shape[0]):
        h = _layer(h, p[i], pre_mix_norm[i], w_in[i], b_fgate[i], conv_w[i], conv_b[i], dt_bias[i],
                   a_log[i], d_skip[i], attn_out_norm[i], ssm_out_norm[i], w_out[i],
                   post_mix_norm[i], pre_ffn_norm[i], w_router[i], b_router[i], w_gate[i], b_gate[i],
                   w_up[i], b_up[i], w_down[i], b_down[i], post_ffn_norm[i], ple_norm[i],
                   w_ple_gate[i], w_ple_proj[i], ple_post_norm[i])
    return h
```

```python
import functools

import jax
import jax.numpy as jnp
from jax import lax
from jax.experimental import pallas as pl
from jax.experimental.pallas import tpu as pltpu

F32 = jnp.float32
BF16 = jnp.bfloat16

ATT_HEADS = 8
ATT_HEAD_DIM = 128
SSM_HEAD_DIM = 64
SSM_GROUPS = 8
SSM_STATE = 128
SSM_CHUNK = 128
CONV_WIDTH = 4
N_EXPERTS = 32
TOP_K = 4
SWIGLU_LIMIT = 7.0
SWIGLU_ALPHA = 1.702
EPS = 1e-6
LOG2E = 1.4426950408889634

LANES = 128
SUBLANES = 8
VMEM_LIMIT_BYTES = 56 * 1024 * 1024

MOE_ROWS = 512


def _params(*semantics):
    return pltpu.CompilerParams(dimension_semantics=semantics, vmem_limit_bytes=VMEM_LIMIT_BYTES)


def _rms(x, gain):
    return x * lax.rsqrt(jnp.mean(x * x, axis=-1, keepdims=True) + EPS) * gain


def _sigmoid(x):
    return 1.0 / (1.0 + jnp.exp(-x))


def _softplus(x):
    return jnp.maximum(x, 0.0) + jnp.log1p(jnp.exp(-jnp.abs(x)))


def _split3(x):
    x1 = x.astype(BF16)
    r1 = x - x1.astype(F32)
    x2 = r1.astype(BF16)
    x3 = (r1 - x2.astype(F32)).astype(BF16)
    return x1, x2, x3


def _dot(a, b):
    return jnp.dot(a, b, preferred_element_type=F32)


def _dot_exact_rhs(x, m):
    x1, x2, x3 = _split3(x)
    return _dot(x1, m) + _dot(x2, m) + _dot(x3, m)


def _dot_exact_lhs(m, x):
    x1, x2, x3 = _split3(x)
    return _dot(m, x1) + _dot(m, x2) + _dot(m, x3)


def _split_weight(w):
    hi = w.astype(BF16)
    return jnp.concatenate([hi, (w - hi.astype(F32)).astype(BF16)], axis=1)


def _dot_split(x, w_cat):
    n = w_cat.shape[1] // 2
    x_hi = x.astype(BF16)
    x_lo = (x - x_hi.astype(F32)).astype(BF16)
    a = _dot(x_hi, w_cat)
    return a[:, :n] + a[:, n:] + _dot(x_lo, w_cat[:, :n])


def _store_slabs(ref, base, value):
    rows, width = value.shape
    n = width // LANES
    for c in range(n):
        ref[pl.ds(base + c, rows, stride=n), :] = value[:, c * LANES:(c + 1) * LANES]


PACKED_SLAB = 2 * LANES


def _pack_bf16_pairs(x):
    half = x.shape[1] // 2
    as_bits = lambda v: lax.bitcast_convert_type(v.astype(BF16).astype(F32), jnp.uint32)
    return as_bits(x[:, :half]) | (as_bits(x[:, half:]) >> 16)


def _unpack_bf16_pairs(u):
    hi = lax.bitcast_convert_type(u & jnp.uint32(0xFFFF0000), F32)
    lo = lax.bitcast_convert_type(u << 16, F32)
    return hi.astype(BF16), lo.astype(BF16)


def _load_slabs(ref, base, rows, n):
    return jnp.concatenate([ref[pl.ds(base + c, rows, stride=n), :] for c in range(n)], axis=1)


def _rms_matmul_kernel(x_ref, g_ref, w_ref, o_ref, xn_ref):
    @pl.when(pl.program_id(1) == 0)
    def _():
        xn_ref[...] = _rms(x_ref[...], g_ref[...]).astype(xn_ref.dtype)

    if xn_ref.dtype == F32:
        acc = _dot_split(xn_ref[...], w_ref[...])
    else:
        acc = _dot(xn_ref[...], w_ref[...])
    o_ref[...] = acc.astype(o_ref.dtype)


def _rms_matmul(x, gain, w, out_dtype, tm, tn, split=False):
    m, k = x.shape
    n = w.shape[1] // 2 if split else w.shape[1]
    assert not split or n == tn
    return pl.pallas_call(
        _rms_matmul_kernel,
        grid=(m // tm, n // tn),
        in_specs=[
            pl.BlockSpec((tm, k), lambda i, j: (i, 0)),
            pl.BlockSpec((1, k), lambda i, j: (0, 0)),
            pl.BlockSpec((k, 2 * tn if split else tn), lambda i, j: (0, j)),
        ],
        out_specs=pl.BlockSpec((tm, tn), lambda i, j: (i, j)),
        out_shape=jax.ShapeDtypeStruct((m, n), out_dtype),
        scratch_shapes=[pltpu.VMEM((tm, k), F32 if split else BF16)],
        compiler_params=_params("parallel", "arbitrary"),
    )(x, gain.reshape(1, k), w)


def _fg_cumsum_kernel(fg_ref, b_ref, c_ref):
    seq = fg_ref.shape[2]
    row = lax.broadcasted_iota(jnp.int32, (LANES, LANES), 0)
    col = lax.broadcasted_iota(jnp.int32, (LANES, LANES), 1)
    triu = (row <= col).astype(BF16)
    carry = jnp.zeros((fg_ref.shape[1], 1), F32)
    for ci in range(seq // LANES):
        z = fg_ref[0, :, ci * LANES:(ci + 1) * LANES] + b_ref[...]
        ls = jnp.minimum(z, 0.0) - jnp.log1p(jnp.exp(-jnp.abs(z)))
        cs = _dot_exact_rhs(ls, triu) + carry
        c_ref[0, :, ci * LANES:(ci + 1) * LANES] = cs * LOG2E
        carry = cs[:, LANES - 1:LANES]


def _fg_cumsum(fg_t, b_fgate):
    bsz, heads, seq = fg_t.shape
    return pl.pallas_call(
        _fg_cumsum_kernel,
        grid=(bsz,),
        in_specs=[
            pl.BlockSpec((1, heads, seq), lambda b: (b, 0, 0)),
            pl.BlockSpec((heads, 1), lambda b: (0, 0)),
        ],
        out_specs=pl.BlockSpec((1, heads, seq), lambda b: (b, 0, 0)),
        out_shape=jax.ShapeDtypeStruct((bsz, heads, seq), F32),
        compiler_params=_params("parallel"),
    )(fg_t, b_fgate.reshape(heads, 1))


def _attn_kernel(q_ref, k_ref, v_ref, ck_ref, o_ref, *, blk):
    qi = pl.program_id(2)
    q = q_ref[0]

    def step(j, carry, masked):
        m, l, acc = carry
        start = pl.multiple_of(j * blk, blk)
        kj = k_ref[0, pl.ds(start, blk), :]
        vj = v_ref[0, pl.ds(start, blk), :]
        s = lax.dot_general(q, kj, (((1,), (1,)), ((), ())), preferred_element_type=F32)
        s = s - ck_ref[0, 0, j]
        if masked:
            row = lax.broadcasted_iota(jnp.int32, (blk, blk), 0)
            col = lax.broadcasted_iota(jnp.int32, (blk, blk), 1)
            s = jnp.where(col <= row, s, -jnp.inf)
        m_new = jnp.maximum(m, jnp.max(s, axis=-1, keepdims=True))
        alpha = jnp.exp2(m - m_new)
        p = jnp.exp2(s - m_new)
        l = alpha * l + jnp.sum(p, axis=-1, keepdims=True)
        acc = alpha * acc + _dot(p.astype(BF16), vj)
        return m_new, l, acc

    init = (jnp.full((blk, 1), -jnp.inf, F32), jnp.zeros((blk, 1), F32),
            jnp.zeros((blk, q.shape[-1]), F32))
    carry = lax.fori_loop(0, qi, lambda j, c: step(j, c, False), init)
    _, l, acc = step(qi, carry, True)
    o_ref[0] = (acc / l).astype(o_ref.dtype)


def _fox_attention(qkv, c_t, blk):
    bsz, seq, _ = qkv.shape
    h, dh = ATT_HEADS, ATT_HEAD_DIM
    ck = c_t.reshape(bsz, h, seq // blk, 1, blk)
    return pl.pallas_call(
        functools.partial(_attn_kernel, blk=blk),
        grid=(bsz, h, seq // blk),
        in_specs=[
            pl.BlockSpec((1, blk, dh), lambda b, hh, i: (b, i, hh)),
            pl.BlockSpec((1, seq, dh), lambda b, hh, i: (b, 0, h + hh)),
            pl.BlockSpec((1, seq, dh), lambda b, hh, i: (b, 0, 2 * h + hh)),
            pl.BlockSpec((1, 1, seq // blk, 1, blk), lambda b, hh, i: (b, hh, 0, 0, 0)),
        ],
        out_specs=pl.BlockSpec((1, blk, dh), lambda b, hh, i: (b, i, hh)),
        out_shape=jax.ShapeDtypeStruct((bsz, seq, h * dh), BF16),
        compiler_params=_params("parallel", "parallel", "arbitrary"),
    )(qkv, qkv, qkv, ck)


def _ssd_kernel(xbc_ref, z_ref, dt_ref, dtt_ref, cw_ref, cb_ref, dtb_ref, dtbt_ref, alog_ref,
                alogt_ref, dskip_ref, gain_ref, expand_ref, y_ref, ext_ref, act_ref, state_ref):
    chunk = SSM_CHUNK
    width = y_ref.shape[2]
    heads = dt_ref.shape[2]
    per_group = heads // SSM_GROUPS
    gw = per_group * SSM_HEAD_DIM
    nstate = SSM_STATE
    halo = SUBLANES
    c_idx = pl.program_id(1)

    @pl.when(c_idx == 0)
    def _():
        ext_ref[0:halo, :] = jnp.zeros((halo, ext_ref.shape[1]), F32)
        state_ref[...] = jnp.zeros(state_ref.shape, F32)

    ext_ref[halo:halo + chunk, :] = xbc_ref[0]
    conv_ch = ext_ref.shape[1]
    col_tile = 4 * LANES
    for ct in range(conv_ch // col_tile):
        sl = slice(ct * col_tile, (ct + 1) * col_tile)
        acc = cb_ref[:, sl] + cw_ref[CONV_WIDTH - 1:CONV_WIDTH, sl] * ext_ref[halo:halo + chunk, sl]
        for kk in range(CONV_WIDTH - 1):
            off = halo - (CONV_WIDTH - 1) + kk
            acc = acc + cw_ref[kk:kk + 1, sl] * ext_ref[off:off + chunk, sl]
        act_ref[:, sl] = acc * _sigmoid(acc)
    ext_ref[0:halo, :] = xbc_ref[0, chunk - halo:chunk, :]

    row = lax.broadcasted_iota(jnp.int32, (chunk, chunk), 0)
    col = lax.broadcasted_iota(jnp.int32, (chunk, chunk), 1)
    causal = row >= col
    tril = causal.astype(BF16)
    triu = (row <= col).astype(BF16)
    dt = _softplus(dt_ref[0] + dtb_ref[...])
    adt = dt * (-jnp.exp(alog_ref[...]))
    acs = _dot_exact_lhs(tril, adt)
    dtt = _softplus(dtt_ref[0] + dtbt_ref[...])
    acs_t = _dot_exact_rhs(dtt * (-jnp.exp(alogt_ref[...])), triu)

    expand = expand_ref[...]
    dt_x = _dot_exact_rhs(dt, expand)
    acs_x = _dot_exact_rhs(acs, expand)
    last_x = acs_x[chunk - 1:chunk, :]
    xs = act_ref[:, 0:width]
    xc = xs * dt_x
    xc_bf = xc.astype(BF16)
    xcd_bf = (xc * jnp.exp(last_x - acs_x)).astype(BF16)
    in_decay = jnp.exp(acs_x)
    chunk_decay = jnp.exp(last_x)
    lane = lax.broadcasted_iota(jnp.int32, (chunk, LANES), 1)
    low_half = lane < SSM_HEAD_DIM

    for g in range(SSM_GROUPS):
        gsl = slice(g * gw, (g + 1) * gw)
        b_g = act_ref[:, width + g * nstate:width + (g + 1) * nstate]
        c_g = act_ref[:, width + (SSM_GROUPS + g) * nstate:width + (SSM_GROUPS + g + 1) * nstate]
        c_bf = c_g.astype(BF16)
        cb = lax.dot_general(c_bf, b_g.astype(BF16), (((1,), (1,)), ((), ())),
                             preferred_element_type=F32)
        pieces = []
        for pair in range(per_group // 2):
            ms = []
            for r in (2 * pair, 2 * pair + 1):
                hd = g * per_group + r
                seg = acs[:, hd:hd + 1] - acs_t[hd:hd + 1, :]
                decay = jnp.exp(jnp.where(causal, seg, -jnp.inf))
                ms.append((cb * decay).astype(BF16))
            xpair = xc_bf[:, g * gw + pair * LANES:g * gw + (pair + 1) * LANES]
            zero = jnp.zeros_like(xpair)
            rhs = jnp.concatenate([jnp.where(low_half, xpair, zero),
                                   jnp.where(low_half, zero, xpair)], axis=0)
            pieces.append(_dot(jnp.concatenate(ms, axis=1), rhs))
        y_diag = jnp.concatenate(pieces, axis=1)
        state = state_ref[g]
        y_off = _dot(c_bf, state.astype(BF16)) * in_decay[:, gsl]
        y = y_diag + y_off + dskip_ref[:, gsl] * xs[:, gsl]
        zg = z_ref[0, :, gsl]
        y = y * (zg * _sigmoid(zg))
        y = y * lax.rsqrt(jnp.mean(y * y, axis=-1, keepdims=True) + EPS) * gain_ref[:, gsl]
        y_ref[0, :, gsl] = y.astype(y_ref.dtype)
        state_ref[g] = state * chunk_decay[:, gsl] + _dot(b_g.T.astype(BF16), xcd_bf[:, gsl])


def _ssd(xbc, z, dt_raw, dt_raw_t, conv_w, conv_b, dt_bias, a_log, d_skip, gain):
    bsz, seq, conv_ch = xbc.shape
    width = z.shape[2]
    heads = dt_raw.shape[2]
    chunk = SSM_CHUNK
    gw = width // SSM_GROUPS
    expand = (jnp.arange(width)[None, :] // SSM_HEAD_DIM == jnp.arange(heads)[:, None]).astype(BF16)
    dskip_x = jnp.repeat(d_skip, SSM_HEAD_DIM).reshape(1, width)
    const = lambda shape: pl.BlockSpec(shape, lambda b, c: (0,) * len(shape))
    return pl.pallas_call(
        _ssd_kernel,
        grid=(bsz, seq // chunk),
        in_specs=[
            pl.BlockSpec((1, chunk, conv_ch), lambda b, c: (b, c, 0)),
            pl.BlockSpec((1, chunk, width), lambda b, c: (b, c, 0)),
            pl.BlockSpec((1, chunk, heads), lambda b, c: (b, c, 0)),
            pl.BlockSpec((1, heads, chunk), lambda b, c: (b, 0, c)),
            const((CONV_WIDTH, conv_ch)),
            const((1, conv_ch)),
            const((1, heads)),
            const((heads, 1)),
            const((1, heads)),
            const((heads, 1)),
            const((1, width)),
            const((1, width)),
            const((heads, width)),
        ],
        out_specs=pl.BlockSpec((1, chunk, width), lambda b, c: (b, c, 0)),
        out_shape=jax.ShapeDtypeStruct((bsz, seq, width), BF16),
        scratch_shapes=[
            pltpu.VMEM((SUBLANES + chunk, conv_ch), F32),
            pltpu.VMEM((chunk, conv_ch), F32),
            pltpu.VMEM((SSM_GROUPS, SSM_STATE, gw), F32),
        ],
        compiler_params=_params("parallel", "arbitrary"),
    )(xbc, z, dt_raw, dt_raw_t, conv_w, conv_b.reshape(1, conv_ch), dt_bias.reshape(1, heads),
      dt_bias.reshape(heads, 1), a_log.reshape(1, heads), a_log.reshape(heads, 1), dskip_x,
      gain.reshape(1, width), expand)


def _out_proj_kernel(att_ref, y_ref, x_ref, ga_ref, wa_ref, wy_ref, gpost_ref, gpre_ref, wr_ref,
                     br_ref, h_ref, hn_ref, logit_ref):
    att_n = _rms(att_ref[...].astype(F32), ga_ref[...]).astype(BF16)
    mixed = _dot(att_n, wa_ref[...]) + _dot(y_ref[...], wy_ref[...])
    h = x_ref[...] + _rms(mixed, gpost_ref[...])
    h_ref[...] = h
    hn = _rms(h, gpre_ref[...])
    _store_slabs(hn_ref, 0, _pack_bf16_pairs(hn))
    ne = logit_ref.shape[1]
    logit_ref[...] = _dot_split(hn, wr_ref[...])[:, :ne] + br_ref[...]


def _out_proj(att, y, x, attn_gain, w_att, w_y, post_gain, pre_ffn_gain, w_router, b_router, tm):
    t, d = x.shape
    ne = w_router.shape[1]
    rows = lambda width: pl.BlockSpec((tm, width), lambda i: (i, 0))
    whole = lambda a: pl.BlockSpec(a.shape, lambda i: (0, 0), pipeline_mode=pl.Buffered(1))
    w_router_cat = _split_weight(jnp.pad(w_router, ((0, 0), (0, LANES - ne))))
    args = (att, y, x, attn_gain.reshape(1, -1), w_att, w_y, post_gain.reshape(1, d),
            pre_ffn_gain.reshape(1, d), w_router_cat, b_router.reshape(1, ne))
    return pl.pallas_call(
        _out_proj_kernel,
        grid=(t // tm,),
        in_specs=[rows(att.shape[1]), rows(y.shape[1]), rows(d)] + [whole(a) for a in args[3:]],
        out_specs=[rows(d), pl.BlockSpec((tm * (d // PACKED_SLAB), LANES), lambda i: (i, 0)), rows(ne)],
        out_shape=[jax.ShapeDtypeStruct((t, d), F32),
                   jax.ShapeDtypeStruct((t * (d // PACKED_SLAB), LANES), jnp.uint32),
                   jax.ShapeDtypeStruct((t, ne), F32)],
        compiler_params=_params("parallel"),
    )(*args)


def _topk_kernel(logit_ref, e_ref, w_ref):
    l = logit_ref[...]
    ne = l.shape[1]
    lane = lax.broadcasted_iota(jnp.int32, l.shape, 1)
    slot = lax.broadcasted_iota(jnp.int32, e_ref.shape, 1)
    e_out = jnp.zeros(e_ref.shape, jnp.int32)
    p_out = jnp.zeros(w_ref.shape, F32)
    top = None
    for kk in range(TOP_K):
        m = jnp.max(l, axis=-1, keepdims=True)
        idx = jnp.min(jnp.where(l == m, lane, ne), axis=-1, keepdims=True)
        if top is None:
            top = m
        e_out = jnp.where(slot == kk, idx, e_out)
        p_out = jnp.where(slot == kk, jnp.exp(m - top), p_out)
        l = jnp.where(lane == idx, -jnp.inf, l)
    e_ref[...] = e_out
    w_ref[...] = p_out / jnp.sum(p_out, axis=-1, keepdims=True)


def _router_topk(logits, tm):
    t, ne = logits.shape
    return pl.pallas_call(
        _topk_kernel,
        grid=(t // tm,),
        in_specs=[pl.BlockSpec((tm, ne), lambda i: (i, 0))],
        out_specs=[pl.BlockSpec((tm, TOP_K), lambda i: (i, 0))] * 2,
        out_shape=[jax.ShapeDtypeStruct((t, TOP_K), jnp.int32),
                   jax.ShapeDtypeStruct((t, TOP_K), F32)],
        compiler_params=_params("parallel"),
    )(logits)


DMA_ISSUE_UNROLL = 8


def _issue_row_copies(count, row_index, src_ref, dst_ref, sem, n):
    def body(g, carry):
        for u in range(DMA_ISSUE_UNROLL):
            r = g * DMA_ISSUE_UNROLL + u
            pltpu.make_async_copy(src_ref.at[pl.ds(row_index(r) * n, n)],
                                  dst_ref.at[pl.ds(r * n, n)], sem).start(priority=u % 2)
        return carry

    lax.fori_loop(0, count // DMA_ISSUE_UNROLL, body, 0)


def _gather_kernel(tok_ref, first_ref, valid_ref, used_ref, src_ref, o_ref, buf_ref, sem):
    tm = o_ref.shape[0]
    n = buf_ref.shape[1] // tm
    i = pl.program_id(0)
    used = used_ref[0]
    last = tok_ref.shape[0] - 1

    def issue(block, slot):
        first = first_ref[block]
        valid = valid_ref[block]
        token = lambda r: jnp.where(r < valid, tok_ref[jnp.clip(first + r, 0, last)], 0)
        _issue_row_copies(tm, token, src_ref, buf_ref.at[slot], sem.at[slot], n)

    for slot in range(2):
        @pl.when(jnp.logical_and(i % 2 == slot, i < used))
        def _(slot=slot):
            if slot == 0:
                @pl.when(i == 0)
                def _():
                    issue(0, 0)

            @pl.when(i + 1 < used)
            def _():
                issue(i + 1, 1 - slot)

            pltpu.make_async_copy(src_ref.at[pl.ds(0, tm * n)], buf_ref.at[slot], sem.at[slot]).wait()
            first_half, second_half = _unpack_bf16_pairs(_load_slabs(buf_ref.at[slot], 0, tm, n))
            half = o_ref.shape[1] // 2
            o_ref[:, :half] = first_half
            o_ref[:, half:] = second_half

    @pl.when(i >= used)
    def _():
        o_ref[...] = jnp.zeros(o_ref.shape, o_ref.dtype)


def _gather_rows(src_slabs, tok_sorted, block_first, block_valid, n_used, tm, d, out_dtype):
    n = d // PACKED_SLAB
    n_blocks = block_first.shape[0]
    return pl.pallas_call(
        _gather_kernel,
        grid_spec=pltpu.PrefetchScalarGridSpec(
            num_scalar_prefetch=4,
            grid=(n_blocks,),
            in_specs=[pl.BlockSpec(memory_space=pl.ANY)],
            out_specs=pl.BlockSpec((tm, d), lambda i, *_: (i, 0)),
            scratch_shapes=[pltpu.VMEM((2, tm * n, LANES), src_slabs.dtype),
                            pltpu.SemaphoreType.DMA((2,))],
        ),
        out_shape=jax.ShapeDtypeStruct((n_blocks * tm, d), out_dtype),
        compiler_params=_params("arbitrary"),
    )(tok_sorted, block_first, block_valid, n_used, src_slabs)


def _expert_changed(be_ref, i):
    return jnp.logical_or(i == 0, be_ref[i] != be_ref[jnp.maximum(i - 1, 0)])


def _for_valid_rows(valid, tm, compute, zero_fill):
    half = tm // 2

    @pl.when(valid > half)
    def _():
        compute(tm)

    @pl.when(jnp.logical_and(valid > 0, valid <= half))
    def _():
        compute(half)
        zero_fill(half, tm)

    @pl.when(valid == 0)
    def _():
        zero_fill(0, tm)


def _moe_up_kernel(be_ref, valid_ref, x_ref, wg_ref, wu_ref, bg_ref, bu_ref, h_ref, wg_bf, wu_bf):
    i = pl.program_id(1)

    @pl.when(_expert_changed(be_ref, i))
    def _():
        wg_bf[...] = wg_ref[0].astype(BF16)
        wu_bf[...] = wu_ref[0].astype(BF16)

    def compute(rows):
        x = x_ref[0:rows, :]
        gate = jnp.minimum(_dot(x, wg_bf[...]) + bg_ref[0], SWIGLU_LIMIT)
        up = jnp.clip(_dot(x, wu_bf[...]) + bu_ref[0], -SWIGLU_LIMIT, SWIGLU_LIMIT)
        h_ref[0:rows, :] = ((up + 1.0) * (gate * _sigmoid(gate * SWIGLU_ALPHA))).astype(h_ref.dtype)

    def zero_fill(lo, hi):
        h_ref[lo:hi, :] = jnp.zeros((hi - lo, h_ref.shape[1]), h_ref.dtype)

    _for_valid_rows(valid_ref[i], x_ref.shape[0], compute, zero_fill)


def _moe_down_kernel(be_ref, valid_ref, h_ref, wd_ref, bd_ref, y_ref, wd_bf, slab_ref):
    i = pl.program_id(1)
    tm, nt, _ = y_ref.shape

    @pl.when(_expert_changed(be_ref, i))
    def _():
        wd_bf[...] = wd_ref[0].astype(BF16)

    def compute(rows):
        _store_slabs(slab_ref, 0, _dot(h_ref[0:rows, :], wd_bf[...]) + bd_ref[0])
        y_ref[0:rows] = slab_ref[0:rows * nt, :].reshape(rows, nt, LANES)

    def zero_fill(lo, hi):
        y_ref[lo:hi] = jnp.zeros((hi - lo, nt, LANES), y_ref.dtype)

    _for_valid_rows(valid_ref[i], tm, compute, zero_fill)


def _moe_up(xs, block_e, block_valid, w_gate, b_gate, w_up, b_up, tm, tn):
    n_pad, d = xs.shape
    ne, _, ff = w_gate.shape
    w_spec = pl.BlockSpec((1, d, tn), lambda j, i, be, nu: (be[i], 0, j))
    b_spec = pl.BlockSpec((1, 1, tn), lambda j, i, be, nu: (be[i], 0, j))
    return pl.pallas_call(
        _moe_up_kernel,
        grid_spec=pltpu.PrefetchScalarGridSpec(
            num_scalar_prefetch=2,
            grid=(ff // tn, n_pad // tm),
            in_specs=[pl.BlockSpec((tm, d), lambda j, i, be, nu: (i, 0)),
                      w_spec, w_spec, b_spec, b_spec],
            out_specs=pl.BlockSpec((tm, tn), lambda j, i, be, nu: (i, j)),
            scratch_shapes=[pltpu.VMEM((d, tn), BF16), pltpu.VMEM((d, tn), BF16)],
        ),
        out_shape=jax.ShapeDtypeStruct((n_pad, ff), BF16),
        compiler_params=_params("arbitrary", "arbitrary"),
    )(block_e, block_valid, xs, w_gate, w_up, b_gate.reshape(ne, 1, ff), b_up.reshape(ne, 1, ff))


def _moe_down(hs, block_e, block_valid, w_down, b_down, tm, tn):
    n_pad, ff = hs.shape
    ne, _, d = w_down.shape
    nt = tn // LANES
    y = pl.pallas_call(
        _moe_down_kernel,
        grid_spec=pltpu.PrefetchScalarGridSpec(
            num_scalar_prefetch=2,
            grid=(d // tn, n_pad // tm),
            in_specs=[pl.BlockSpec((tm, ff), lambda j, i, be, nu: (i, 0)),
                      pl.BlockSpec((1, ff, tn), lambda j, i, be, nu: (be[i], 0, j)),
                      pl.BlockSpec((1, 1, tn), lambda j, i, be, nu: (be[i], 0, j))],
            out_specs=pl.BlockSpec((tm, nt, LANES), lambda j, i, be, nu: (i, j, 0)),
            scratch_shapes=[pltpu.VMEM((ff, tn), BF16), pltpu.VMEM((tm * nt, LANES), F32)],
        ),
        out_shape=jax.ShapeDtypeStruct((n_pad, d // LANES, LANES), F32),
        compiler_params=_params("arbitrary", "arbitrary"),
    )(block_e, block_valid, hs, w_down, b_down.reshape(ne, 1, d))
    return y.reshape(n_pad * (d // LANES), LANES)


def _combine_ple_kernel(pos_ref, pos_next_ref, y_ref, w_ref, h_ref, p_ref, gffn_ref, gple_ref,
                        wgate_ref, wproj_ref, gpost_ref, o_ref, buf_ref, ff_ref, sem):
    tm, d = h_ref.shape
    n = d // LANES
    i = pl.program_id(0)

    def issue(idx_ref, slot):
        _issue_row_copies(TOP_K * tm, lambda r: idx_ref[0, 0, r], y_ref, buf_ref.at[slot],
                          sem.at[slot], n)

    for slot in range(2):
        @pl.when(i % 2 == slot)
        def _(slot=slot):
            if slot == 0:
                @pl.when(i == 0)
                def _():
                    issue(pos_ref, 0)

            @pl.when(i + 1 < pl.num_programs(0))
            def _():
                issue(pos_next_ref, 1 - slot)

            pltpu.make_async_copy(y_ref.at[pl.ds(0, TOP_K * tm * n)], buf_ref.at[slot],
                                  sem.at[slot]).wait()
            w = w_ref[...]
            ff = w[:, 0:1] * _load_slabs(buf_ref.at[slot], 0, tm, n)
            for kk in range(1, TOP_K):
                ff = ff + w[:, kk:kk + 1] * _load_slabs(buf_ref.at[slot], kk * tm * n, tm, n)
            ff_ref[...] = ff

    h = h_ref[...] + _rms(ff_ref[...], gffn_ref[...])
    gate = _sigmoid(_dot(_rms(h, gple_ref[...]).astype(BF16), wgate_ref[...]))
    emb = _dot(p_ref[...].astype(BF16), wproj_ref[...])
    o_ref[...] = h + _rms(gate * emb, gpost_ref[...])


def _combine_ple(y_rows, pos, top_w, h, p, ffn_gain, ple_gain, w_ple_gate, w_ple_proj,
                 ple_post_gain, tm):
    t, d = h.shape
    rows = lambda width: pl.BlockSpec((tm, width), lambda i: (i, 0))
    whole = lambda a: pl.BlockSpec(a.shape, lambda i: (0, 0), pipeline_mode=pl.Buffered(1))
    consts = (ffn_gain.reshape(1, d), ple_gain.reshape(1, d), w_ple_gate, w_ple_proj,
              ple_post_gain.reshape(1, d))
    steps = t // tm
    return pl.pallas_call(
        _combine_ple_kernel,
        grid=(steps,),
        in_specs=[
            pl.BlockSpec((1, 1, TOP_K * tm), lambda i: (i, 0, 0), memory_space=pltpu.SMEM),
            pl.BlockSpec((1, 1, TOP_K * tm), lambda i: (jnp.minimum(i + 1, steps - 1), 0, 0),
                         memory_space=pltpu.SMEM),
            pl.BlockSpec(memory_space=pl.ANY),
            rows(TOP_K), rows(d), rows(p.shape[1]),
        ] + [whole(a) for a in consts],
        out_specs=rows(d),
        out_shape=jax.ShapeDtypeStruct((t, d), F32),
        scratch_shapes=[pltpu.VMEM((2, TOP_K * tm * (d // LANES), LANES), F32),
                        pltpu.VMEM((tm, d), F32), pltpu.SemaphoreType.DMA((2,))],
        compiler_params=_params("arbitrary"),
    )(pos, pos, y_rows, top_w, h, p, *consts)


def _routing_tables(top_e, tm, tok_block):
    t = top_e.shape[0]
    n = t * TOP_K
    n_blocks = n // tm + N_EXPERTS
    e_flat = top_e.reshape(n)
    experts = jnp.arange(N_EXPERTS, dtype=jnp.int32)
    order = jnp.argsort(e_flat, stable=True).astype(jnp.int32)
    rank = jnp.argsort(order).astype(jnp.int32)
    onehot = e_flat[:, None] == experts[None, :]
    counts = jnp.sum(onehot, axis=0, dtype=jnp.int32)
    group_start = jnp.cumsum(counts) - counts
    padded = ((counts + tm - 1) // tm) * tm
    padded_end = jnp.cumsum(padded)
    padded_start = padded_end - padded
    shift = padded_start - group_start
    pos = (rank + jnp.sum(jnp.where(onehot, shift[None, :], 0), axis=1)).reshape(t, TOP_K)
    n_used = (padded_end[-1] // tm).astype(jnp.int32).reshape(1)
    offs = jnp.arange(n_blocks, dtype=jnp.int32) * tm
    block_e = jnp.sum(padded_end[None, :] <= offs[:, None], axis=1, dtype=jnp.int32)
    last_e = jnp.sum(padded_end <= (n_used[0] - 1) * tm, dtype=jnp.int32)
    block_e = jnp.minimum(jnp.where(offs < padded_end[-1], block_e, last_e), N_EXPERTS - 1)
    sel = block_e[:, None] == experts[None, :]
    pick = lambda table: jnp.sum(jnp.where(sel, table[None, :], 0), axis=1)
    in_group = offs - pick(padded_start)
    block_first = (pick(group_start) + in_group).astype(jnp.int32)
    block_valid = jnp.where(offs < padded_end[-1], jnp.clip(pick(counts) - in_group, 0, tm), 0)
    pos_blocks = pos.reshape(t // tok_block, tok_block, TOP_K).transpose(0, 2, 1)
    return (order // TOP_K, block_first, block_valid.astype(jnp.int32),
            pos_blocks.reshape(t // tok_block, 1, TOP_K * tok_block), block_e, n_used)


def _layer(h, p, pre_mix_norm, w_in, b_fgate, conv_w, conv_b, dt_bias, a_log, d_skip,
           attn_out_norm, ssm_out_norm, w_out, post_mix_norm, pre_ffn_norm, w_router, b_router,
           w_gate, b_gate, w_up, b_up, w_down, b_down, post_ffn_norm, ple_norm, w_ple_gate,
           w_ple_proj, ple_post_norm):
    bsz, seq, d = h.shape
    t = bsz * seq
    att_w = ATT_HEADS * ATT_HEAD_DIM
    heads = dt_bias.shape[0]
    ssm_w = heads * SSM_HEAD_DIM
    conv_ch = conv_w.shape[1]
    o_fg = 3 * att_w
    o_z = o_fg + ATT_HEADS
    o_xbc = o_z + ssm_w
    o_dt = o_xbc + conv_ch
    x2 = h.reshape(t, d)

    q_scale = LOG2E * ATT_HEAD_DIM ** -0.5
    w_qkv = jnp.concatenate([w_in[:, :att_w] * q_scale, w_in[:, att_w:o_fg]], axis=1).astype(BF16)
    w_z = w_in[:, o_z:o_xbc].astype(BF16)
    w_xbc = w_in[:, o_xbc:o_dt].astype(BF16)
    small_pad = LANES - ATT_HEADS - heads
    w_small = jnp.concatenate([w_in[:, o_fg:o_z], w_in[:, o_dt:], jnp.zeros((d, small_pad), F32)], axis=1)

    tm = min(1024, t)
    qkv = _rms_matmul(x2, pre_mix_norm, w_qkv, BF16, tm, 1024)
    zz = _rms_matmul(x2, pre_mix_norm, w_z, F32, tm, 1024)
    xbc = _rms_matmul(x2, pre_mix_norm, w_xbc, F32, tm, 1024)
    small = _rms_matmul(x2, pre_mix_norm, _split_weight(w_small), F32, tm, LANES, split=True)
    small = small.reshape(bsz, seq, LANES)

    fg_t = jnp.swapaxes(small[:, :, :ATT_HEADS], 1, 2)
    dt_raw = small[:, :, ATT_HEADS:ATT_HEADS + heads]
    c_t = _fg_cumsum(fg_t, b_fgate)
    att = _fox_attention(qkv.reshape(bsz, seq, 3 * att_w), c_t, min(512, seq))
    y = _ssd(xbc.reshape(bsz, seq, conv_ch), zz.reshape(bsz, seq, ssm_w), dt_raw,
             jnp.swapaxes(dt_raw, 1, 2), conv_w, conv_b, dt_bias, a_log, d_skip, ssm_out_norm)

    w_out_bf = w_out.astype(BF16)
    h1, hn, logits = _out_proj(att.reshape(t, att_w), y.reshape(t, ssm_w), x2, attn_out_norm,
                               w_out_bf[:att_w], w_out_bf[att_w:], post_mix_norm, pre_ffn_norm,
                               w_router, b_router, min(512, t))
    top_e, top_w = _router_topk(logits, min(2048, t))

    tok_block = min(256, t)
    tok_sorted, block_first, block_valid, pos, block_e, n_used = _routing_tables(
        top_e, MOE_ROWS, tok_block)
    xs = _gather_rows(hn, tok_sorted, block_first, block_valid, n_used, MOE_ROWS, d, BF16)
    hs = _moe_up(xs, block_e, block_valid, w_gate, b_gate, w_up, b_up, MOE_ROWS, min(1024, w_gate.shape[2]))
    y_rows = _moe_down(hs, block_e, block_valid, w_down, b_down, MOE_ROWS, min(1024, d))
    out = _combine_ple(y_rows, pos, top_w, h1, p.reshape(t, -1), post_ffn_norm, ple_norm,
                       w_ple_gate.astype(BF16), w_ple_proj.astype(BF16), ple_post_norm, tok_block)
    return out.reshape(bsz, seq, d)


def kernel(x, p, pre_mix_norm, w_in, b_fgate, conv_w, conv_b, dt_bias, a_log, d_skip, attn_out_norm, ssm_out_norm, w_out, post_mix_norm, pre_ffn_norm, w_router, b_router, w_gate, b_gate, w_up, b_up, w_down, b_down, post_ffn_norm, ple_norm, w_ple_gate, w_ple_proj, ple_post_norm):
    h = x
    for i in range(p.shape[0]):
        h = _layer(h, p[i], pre_mix_norm[i], w_in[i], b_fgate[i], conv_w[i], conv_b[i], dt_bias[i],
                   a_log[i], d_skip[i], attn_out_norm[i], ssm_out_norm[i], w_out[i],
                   post_mix_norm[i], pre_ffn_norm[i], w_router[i], b_router[i], w_gate[i], b_gate[i],
                   w_up[i], b_up[i], w_down[i], b_down[i], post_ffn_norm[i], ple_norm[i],
                   w_ple_gate[i], w_ple_proj[i], ple_post_norm[i])
    return h
```

```python
import functools

import jax
import jax.numpy as jnp
from jax import lax
from jax.experimental import pallas as pl
from jax.experimental.pallas import tpu as pltpu

F32 = jnp.float32
BF16 = jnp.bfloat16

ATT_HEADS = 8
ATT_HEAD_DIM = 128
SSM_HEAD_DIM = 64
SSM_GROUPS = 8
SSM_STATE = 128
SSM_CHUNK = 128
CONV_WIDTH = 4
N_EXPERTS = 32
TOP_K = 4
SWIGLU_LIMIT = 7.0
SWIGLU_ALPHA = 1.702
EPS = 1e-6
LOG2E = 1.4426950408889634

LANES = 128
SUBLANES = 8
VMEM_LIMIT_BYTES = 56 * 1024 * 1024

MOE_ROWS = 512


def _params(*semantics):
    return pltpu.CompilerParams(dimension_semantics=semantics, vmem_limit_bytes=VMEM_LIMIT_BYTES)


def _rms(x, gain):
    return x * lax.rsqrt(jnp.mean(x * x, axis=-1, keepdims=True) + EPS) * gain


def _sigmoid(x):
    return 1.0 / (1.0 + jnp.exp(-x))


def _softplus(x):
    return jnp.maximum(x, 0.0) + jnp.log1p(jnp.exp(-jnp.abs(x)))


def _split3(x):
    x1 = x.astype(BF16)
    r1 = x - x1.astype(F32)
    x2 = r1.astype(BF16)
    x3 = (r1 - x2.astype(F32)).astype(BF16)
    return x1, x2, x3


def _dot(a, b):
    return jnp.dot(a, b, preferred_element_type=F32)


def _dot_exact_rhs(x, m):
    x1, x2, x3 = _split3(x)
    return _dot(x1, m) + _dot(x2, m) + _dot(x3, m)


def _dot_exact_lhs(m, x):
    x1, x2, x3 = _split3(x)
    return _dot(m, x1) + _dot(m, x2) + _dot(m, x3)


def _split_weight(w):
    hi = w.astype(BF16)
    return jnp.concatenate([hi, (w - hi.astype(F32)).astype(BF16)], axis=1)


def _dot_split(x, w_cat):
    n = w_cat.shape[1] // 2
    x_hi = x.astype(BF16)
    x_lo = (x - x_hi.astype(F32)).astype(BF16)
    a = _dot(x_hi, w_cat)
    return a[:, :n] + a[:, n:] + _dot(x_lo, w_cat[:, :n])


def _store_slabs(ref, base, value):
    rows, width = value.shape
    n = width // LANES
    for c in range(n):
        ref[pl.ds(base + c, rows, stride=n), :] = value[:, c * LANES:(c + 1) * LANES]


PACKED_SLAB = 2 * LANES


def _pack_bf16_pairs(x):
    half = x.shape[1] // 2
    as_bits = lambda v: lax.bitcast_convert_type(v.astype(BF16).astype(F32), jnp.uint32)
    return as_bits(x[:, :half]) | (as_bits(x[:, half:]) >> 16)


def _unpack_bf16_pairs(u):
    hi = lax.bitcast_convert_type(u & jnp.uint32(0xFFFF0000), F32)
    lo = lax.bitcast_convert_type(u << 16, F32)
    return hi.astype(BF16), lo.astype(BF16)


def _load_slabs(ref, base, rows, n):
    return jnp.concatenate([ref[pl.ds(base + c, rows, stride=n), :] for c in range(n)], axis=1)


def _rms_matmul_kernel(x_ref, g_ref, w_ref, o_ref, xn_ref):
    @pl.when(pl.program_id(1) == 0)
    def _():
        xn_ref[...] = _rms(x_ref[...], g_ref[...]).astype(xn_ref.dtype)

    if xn_ref.dtype == F32:
        acc = _dot_split(xn_ref[...], w_ref[...])
    else:
        acc = _dot(xn_ref[...], w_ref[...])
    o_ref[...] = acc.astype(o_ref.dtype)


def _rms_matmul(x, gain, w, out_dtype, tm, tn, split=False):
    m, k = x.shape
    n = w.shape[1] // 2 if split else w.shape[1]
    assert not split or n == tn
    return pl.pallas_call(
        _rms_matmul_kernel,
        grid=(m // tm, n // tn),
        in_specs=[
            pl.BlockSpec((tm, k), lambda i, j: (i, 0)),
            pl.BlockSpec((1, k), lambda i, j: (0, 0)),
            pl.BlockSpec((k, 2 * tn if split else tn), lambda i, j: (0, j)),
        ],
        out_specs=pl.BlockSpec((tm, tn), lambda i, j: (i, j)),
        out_shape=jax.ShapeDtypeStruct((m, n), out_dtype),
        scratch_shapes=[pltpu.VMEM((tm, k), F32 if split else BF16)],
        compiler_params=_params("parallel", "arbitrary"),
    )(x, gain.reshape(1, k), w)


def _fg_cumsum_kernel(fg_ref, b_ref, c_ref):
    seq = fg_ref.shape[2]
    row = lax.broadcasted_iota(jnp.int32, (LANES, LANES), 0)
    col = lax.broadcasted_iota(jnp.int32, (LANES, LANES), 1)
    triu = (row <= col).astype(BF16)
    carry = jnp.zeros((fg_ref.shape[1], 1), F32)
    for ci in range(seq // LANES):
        z = fg_ref[0, :, ci * LANES:(ci + 1) * LANES] + b_ref[...]
        ls = jnp.minimum(z, 0.0) - jnp.log1p(jnp.exp(-jnp.abs(z)))
        cs = _dot_exact_rhs(ls, triu) + carry
        c_ref[0, :, ci * LANES:(ci + 1) * LANES] = cs * LOG2E
        carry = cs[:, LANES - 1:LANES]


def _fg_cumsum(fg_t, b_fgate):
    bsz, heads, seq = fg_t.shape
    return pl.pallas_call(
        _fg_cumsum_kernel,
        grid=(bsz,),
        in_specs=[
            pl.BlockSpec((1, heads, seq), lambda b: (b, 0, 0)),
            pl.BlockSpec((heads, 1), lambda b: (0, 0)),
        ],
        out_specs=pl.BlockSpec((1, heads, seq), lambda b: (b, 0, 0)),
        out_shape=jax.ShapeDtypeStruct((bsz, heads, seq), F32),
        compiler_params=_params("parallel"),
    )(fg_t, b_fgate.reshape(heads, 1))


def _attn_kernel(q_ref, k_ref, v_ref, ck_ref, o_ref, *, blk):
    qi = pl.program_id(2)
    q = q_ref[0]

    def step(j, carry, masked):
        m, l, acc = carry
        start = pl.multiple_of(j * blk, blk)
        kj = k_ref[0, pl.ds(start, blk), :]
        vj = v_ref[0, pl.ds(start, blk), :]
        s = lax.dot_general(q, kj, (((1,), (1,)), ((), ())), preferred_element_type=F32)
        s = s - ck_ref[0, 0, j]
        if masked:
            row = lax.broadcasted_iota(jnp.int32, (blk, blk), 0)
            col = lax.broadcasted_iota(jnp.int32, (blk, blk), 1)
            s = jnp.where(col <= row, s, -jnp.inf)
        m_new = jnp.maximum(m, jnp.max(s, axis=-1, keepdims=True))
        alpha = jnp.exp2(m - m_new)
        p = jnp.exp2(s - m_new)
        l = alpha * l + jnp.sum(p, axis=-1, keepdims=True)
        acc = alpha * acc + _dot(p.astype(BF16), vj)
        return m_new, l, acc

    init = (jnp.full((blk, 1), -jnp.inf, F32), jnp.zeros((blk, 1), F32),
            jnp.zeros((blk, q.shape[-1]), F32))
    carry = lax.fori_loop(0, qi, lambda j, c: step(j, c, False), init)
    _, l, acc = step(qi, carry, True)
    o_ref[0] = (acc / l).astype(o_ref.dtype)


def _fox_attention(qkv, c_t, blk):
    bsz, seq, _ = qkv.shape
    h, dh = ATT_HEADS, ATT_HEAD_DIM
    ck = c_t.reshape(bsz, h, seq // blk, 1, blk)
    return pl.pallas_call(
        functools.partial(_attn_kernel, blk=blk),
        grid=(bsz, h, seq // blk),
        in_specs=[
            pl.BlockSpec((1, blk, dh), lambda b, hh, i: (b, i, hh)),
            pl.BlockSpec((1, seq, dh), lambda b, hh, i: (b, 0, h + hh)),
            pl.BlockSpec((1, seq, dh), lambda b, hh, i: (b, 0, 2 * h + hh)),
            pl.BlockSpec((1, 1, seq // blk, 1, blk), lambda b, hh, i: (b, hh, 0, 0, 0)),
        ],
        out_specs=pl.BlockSpec((1, blk, dh), lambda b, hh, i: (b, i, hh)),
        out_shape=jax.ShapeDtypeStruct((bsz, seq, h * dh), BF16),
        compiler_params=_params("parallel", "parallel", "arbitrary"),
    )(qkv, qkv, qkv, ck)


def _ssd_kernel(xbc_ref, z_ref, dt_ref, dtt_ref, cw_ref, cb_ref, dtb_ref, dtbt_ref, alog_ref,
                alogt_ref, dskip_ref, gain_ref, expand_ref, y_ref, ext_ref, act_ref, state_ref):
    chunk = SSM_CHUNK
    width = y_ref.shape[2]
    heads = dt_ref.shape[2]
    per_group = heads // SSM_GROUPS
    gw = per_group * SSM_HEAD_DIM
    nstate = SSM_STATE
    halo = SUBLANES
    c_idx = pl.program_id(1)

    @pl.when(c_idx == 0)
    def _():
        ext_ref[0:halo, :] = jnp.zeros((halo, ext_ref.shape[1]), F32)
        state_ref[...] = jnp.zeros(state_ref.shape, F32)

    ext_ref[halo:halo + chunk, :] = xbc_ref[0]
    conv_ch = ext_ref.shape[1]
    col_tile = 4 * LANES
    for ct in range(conv_ch // col_tile):
        sl = slice(ct * col_tile, (ct + 1) * col_tile)
        acc = cb_ref[:, sl] + cw_ref[CONV_WIDTH - 1:CONV_WIDTH, sl] * ext_ref[halo:halo + chunk, sl]
        for kk in range(CONV_WIDTH - 1):
            off = halo - (CONV_WIDTH - 1) + kk
            acc = acc + cw_ref[kk:kk + 1, sl] * ext_ref[off:off + chunk, sl]
        act_ref[:, sl] = acc * _sigmoid(acc)
    ext_ref[0:halo, :] = xbc_ref[0, chunk - halo:chunk, :]

    row = lax.broadcasted_iota(jnp.int32, (chunk, chunk), 0)
    col = lax.broadcasted_iota(jnp.int32, (chunk, chunk), 1)
    causal = row >= col
    tril = causal.astype(BF16)
    triu = (row <= col).astype(BF16)
    dt = _softplus(dt_ref[0] + dtb_ref[...])
    adt = dt * (-jnp.exp(alog_ref[...]))
    acs = _dot_exact_lhs(tril, adt)
    dtt = _softplus(dtt_ref[0] + dtbt_ref[...])
    acs_t = _dot_exact_rhs(dtt * (-jnp.exp(alogt_ref[...])), triu)

    expand = expand_ref[...]
    dt_x = _dot_exact_rhs(dt, expand)
    acs_x = _dot_exact_rhs(acs, expand)
    last_x = acs_x[chunk - 1:chunk, :]
    xs = act_ref[:, 0:width]
    xc = xs * dt_x
    xc_bf = xc.astype(BF16)
    xcd_bf = (xc * jnp.exp(last_x - acs_x)).astype(BF16)
    in_decay = jnp.exp(acs_x)
    chunk_decay = jnp.exp(last_x)
    lane = lax.broadcasted_iota(jnp.int32, (chunk, LANES), 1)
    low_half = lane < SSM_HEAD_DIM

    for g in range(SSM_GROUPS):
        gsl = slice(g * gw, (g + 1) * gw)
        b_g = act_ref[:, width + g * nstate:width + (g + 1) * nstate]
        c_g = act_ref[:, width + (SSM_GROUPS + g) * nstate:width + (SSM_GROUPS + g + 1) * nstate]
        c_bf = c_g.astype(BF16)
        cb = lax.dot_general(c_bf, b_g.astype(BF16), (((1,), (1,)), ((), ())),
                             preferred_element_type=F32)
        pieces = []
        for pair in range(per_group // 2):
            ms = []
            for r in (2 * pair, 2 * pair + 1):
                hd = g * per_group + r
                seg = acs[:, hd:hd + 1] - acs_t[hd:hd + 1, :]
                decay = jnp.exp(jnp.where(causal, seg, -jnp.inf))
                ms.append((cb * decay).astype(BF16))
            xpair = xc_bf[:, g * gw + pair * LANES:g * gw + (pair + 1) * LANES]
            zero = jnp.zeros_like(xpair)
            rhs = jnp.concatenate([jnp.where(low_half, xpair, zero),
                                   jnp.where(low_half, zero, xpair)], axis=0)
            pieces.append(_dot(jnp.concatenate(ms, axis=1), rhs))
        y_diag = jnp.concatenate(pieces, axis=1)
        state = state_ref[g]
        y_off = _dot(c_bf, state.astype(BF16)) * in_decay[:, gsl]
        y = y_diag + y_off + dskip_ref[:, gsl] * xs[:, gsl]
        zg = z_ref[0, :, gsl]
        y = y * (zg * _sigmoid(zg))
        y = y * lax.rsqrt(jnp.mean(y * y, axis=-1, keepdims=True) + EPS) * gain_ref[:, gsl]
        y_ref[0, :, gsl] = y.astype(y_ref.dtype)
        state_ref[g] = state * chunk_decay[:, gsl] + _dot(b_g.T.astype(BF16), xcd_bf[:, gsl])


def _ssd(xbc, z, dt_raw, dt_raw_t, conv_w, conv_b, dt_bias, a_log, d_skip, gain):
    bsz, seq, conv_ch = xbc.shape
    width = z.shape[2]
    heads = dt_raw.shape[2]
    chunk = SSM_CHUNK
    gw = width // SSM_GROUPS
    expand = (jnp.arange(width)[None, :] // SSM_HEAD_DIM == jnp.arange(heads)[:, None]).astype(BF16)
    dskip_x = jnp.repeat(d_skip, SSM_HEAD_DIM).reshape(1, width)
    const = lambda shape: pl.BlockSpec(shape, lambda b, c: (0,) * len(shape))
    return pl.pallas_call(
        _ssd_kernel,
        grid=(bsz, seq // chunk),
        in_specs=[
            pl.BlockSpec((1, chunk, conv_ch), lambda b, c: (b, c, 0)),
            pl.BlockSpec((1, chunk, width), lambda b, c: (b, c, 0)),
            pl.BlockSpec((1, chunk, heads), lambda b, c: (b, c, 0)),
            pl.BlockSpec((1, heads, chunk), lambda b, c: (b, 0, c)),
            const((CONV_WIDTH, conv_ch)),
            const((1, conv_ch)),
            const((1, heads)),
            const((heads, 1)),
            const((1, heads)),
            const((heads, 1)),
            const((1, width)),
            const((1, width)),
            const((heads, width)),
        ],
        out_specs=pl.BlockSpec((1, chunk, width), lambda b, c: (b, c, 0)),
        out_shape=jax.ShapeDtypeStruct((bsz, seq, width), BF16),
        scratch_shapes=[
            pltpu.VMEM((SUBLANES + chunk, conv_ch), F32),
            pltpu.VMEM((chunk, conv_ch), F32),
            pltpu.VMEM((SSM_GROUPS, SSM_STATE, gw), F32),
        ],
        compiler_params=_params("parallel", "arbitrary"),
    )(xbc, z, dt_raw, dt_raw_t, conv_w, conv_b.reshape(1, conv_ch), dt_bias.reshape(1, heads),
      dt_bias.reshape(heads, 1), a_log.reshape(1, heads), a_log.reshape(heads, 1), dskip_x,
      gain.reshape(1, width), expand)


def _out_proj_kernel(att_ref, y_ref, x_ref, ga_ref, wa_ref, wy_ref, gpost_ref, gpre_ref, wr_ref,
                     br_ref, h_ref, hn_ref, logit_ref):
    att_n = _rms(att_ref[...].astype(F32), ga_ref[...]).astype(BF16)
    mixed = _dot(att_n, wa_ref[...]) + _dot(y_ref[...], wy_ref[...])
    h = x_ref[...] + _rms(mixed, gpost_ref[...])
    h_ref[...] = h
    hn = _rms(h, gpre_ref[...])
    _store_slabs(hn_ref, 0, _pack_bf16_pairs(hn))
    ne = logit_ref.shape[1]
    logit_ref[...] = _dot_split(hn, wr_ref[...])[:, :ne] + br_ref[...]


def _out_proj(att, y, x, attn_gain, w_att, w_y, post_gain, pre_ffn_gain, w_router, b_router, tm):
    t, d = x.shape
    ne = w_router.shape[1]
    rows = lambda width: pl.BlockSpec((tm, width), lambda i: (i, 0))
    whole = lambda a: pl.BlockSpec(a.shape, lambda i: (0, 0), pipeline_mode=pl.Buffered(1))
    w_router_cat = _split_weight(jnp.pad(w_router, ((0, 0), (0, LANES - ne))))
    args = (att, y, x, attn_gain.reshape(1, -1), w_att, w_y, post_gain.reshape(1, d),
            pre_ffn_gain.reshape(1, d), w_router_cat, b_router.reshape(1, ne))
    return pl.pallas_call(
        _out_proj_kernel,
        grid=(t // tm,),
        in_specs=[rows(att.shape[1]), rows(y.shape[1]), rows(d)] + [whole(a) for a in args[3:]],
        out_specs=[rows(d), pl.BlockSpec((tm * (d // PACKED_SLAB), LANES), lambda i: (i, 0)), rows(ne)],
        out_shape=[jax.ShapeDtypeStruct((t, d), F32),
                   jax.ShapeDtypeStruct((t * (d // PACKED_SLAB), LANES), jnp.uint32),
                   jax.ShapeDtypeStruct((t, ne), F32)],
        compiler_params=_params("parallel"),
    )(*args)


def _topk_kernel(logit_ref, e_ref, w_ref):
    l = logit_ref[...]
    ne = l.shape[1]
    lane = lax.broadcasted_iota(jnp.int32, l.shape, 1)
    slot = lax.broadcasted_iota(jnp.int32, e_ref.shape, 1)
    e_out = jnp.zeros(e_ref.shape, jnp.int32)
    p_out = jnp.zeros(w_ref.shape, F32)
    top = None
    for kk in range(TOP_K):
        m = jnp.max(l, axis=-1, keepdims=True)
        idx = jnp.min(jnp.where(l == m, lane, ne), axis=-1, keepdims=True)
        if top is None:
            top = m
        e_out = jnp.where(slot == kk, idx, e_out)
        p_out = jnp.where(slot == kk, jnp.exp(m - top), p_out)
        l = jnp.where(lane == idx, -jnp.inf, l)
    e_ref[...] = e_out
    w_ref[...] = p_out / jnp.sum(p_out, axis=-1, keepdims=True)


def _router_topk(logits, tm):
    t, ne = logits.shape
    return pl.pallas_call(
        _topk_kernel,
        grid=(t // tm,),
        in_specs=[pl.BlockSpec((tm, ne), lambda i: (i, 0))],
        out_specs=[pl.BlockSpec((tm, TOP_K), lambda i: (i, 0))] * 2,
        out_shape=[jax.ShapeDtypeStruct((t, TOP_K), jnp.int32),
                   jax.ShapeDtypeStruct((t, TOP_K), F32)],
        compiler_params=_params("parallel"),
    )(logits)


DMA_ISSUE_UNROLL = 8


def _issue_row_copies(count, row_index, src_ref, dst_ref, sem, n):
    trips = lax.shift_right_logical(count + (DMA_ISSUE_UNROLL - 1), DMA_ISSUE_UNROLL.bit_length() - 1)

    def body(g, carry):
        for u in range(DMA_ISSUE_UNROLL):
            r = g * DMA_ISSUE_UNROLL + u
            pltpu.make_async_copy(src_ref.at[pl.ds(row_index(r) * n, n)],
                                  dst_ref.at[pl.ds(r * n, n)], sem).start(priority=u % 2)
        return carry

    lax.fori_loop(0, trips, body, 0)
    return trips


def _gather_kernel(tok_ref, first_ref, valid_ref, used_ref, src_ref, o_ref, buf_ref, sem):
    tm = o_ref.shape[0]
    n = buf_ref.shape[1] // tm
    i = pl.program_id(0)
    used = used_ref[0]
    last = tok_ref.shape[0] - 1

    group = DMA_ISSUE_UNROLL * n

    def issue(block, slot):
        first = first_ref[block]
        valid = valid_ref[block]
        token = lambda r: jnp.where(r < valid, tok_ref[jnp.clip(first + r, 0, last)], 0)
        _issue_row_copies(valid, token, src_ref, buf_ref.at[slot], sem.at[slot], n)

    def wait_group(g, slot):
        pltpu.make_async_copy(src_ref.at[pl.ds(0, group)], buf_ref.at[slot, pl.ds(g * group, group)],
                              sem.at[slot]).wait()

    @pl.when(i == 0)
    def _():
        buf_ref[...] = jnp.zeros(buf_ref.shape, buf_ref.dtype)

    for slot in range(2):
        @pl.when(jnp.logical_and(i % 2 == slot, i < used))
        def _(slot=slot):
            if slot == 0:
                @pl.when(i == 0)
                def _():
                    issue(0, 0)

            @pl.when(i + 1 < used)
            def _():
                issue(i + 1, 1 - slot)

            trips = lax.shift_right_logical(valid_ref[i] + (DMA_ISSUE_UNROLL - 1),
                                            DMA_ISSUE_UNROLL.bit_length() - 1)
            lax.fori_loop(0, trips, lambda g, c: (wait_group(g, slot), c)[1], 0)
            first_half, second_half = _unpack_bf16_pairs(_load_slabs(buf_ref.at[slot], 0, tm, n))
            half = o_ref.shape[1] // 2
            o_ref[:, :half] = first_half
            o_ref[:, half:] = second_half

    @pl.when(i >= used)
    def _():
        o_ref[...] = jnp.zeros(o_ref.shape, o_ref.dtype)


def _gather_rows(src_slabs, tok_sorted, block_first, block_valid, n_used, tm, d, out_dtype):
    n = d // PACKED_SLAB
    n_blocks = block_first.shape[0]
    return pl.pallas_call(
        _gather_kernel,
        grid_spec=pltpu.PrefetchScalarGridSpec(
            num_scalar_prefetch=4,
            grid=(n_blocks,),
            in_specs=[pl.BlockSpec(memory_space=pl.ANY)],
            out_specs=pl.BlockSpec((tm, d), lambda i, *_: (i, 0)),
            scratch_shapes=[pltpu.VMEM((2, tm * n, LANES), src_slabs.dtype),
                            pltpu.SemaphoreType.DMA((2,))],
        ),
        out_shape=jax.ShapeDtypeStruct((n_blocks * tm, d), out_dtype),
        compiler_params=_params("arbitrary"),
    )(tok_sorted, block_first, block_valid, n_used, src_slabs)


def _expert_changed(be_ref, i):
    return jnp.logical_or(i == 0, be_ref[i] != be_ref[jnp.maximum(i - 1, 0)])


def _weight_run_pipeline(be_ref, run_ref, run_expert_ref, nruns_ref, weights, slots, bf_copies, sem, tn):
    j = pl.program_id(0)
    i = pl.program_id(1)
    n_runs = nruns_ref[0]

    def copies(expert, col, slot):
        return [pltpu.make_async_copy(w.at[expert, :, pl.ds(pl.multiple_of(col * tn, tn), tn)],
                                      s.at[slot], sem.at[slot]) for w, s in zip(weights, slots)]

    @pl.when(_expert_changed(be_ref, i))
    def _():
        run = run_ref[i]
        seq = j * n_runs + run
        slot = lax.rem(seq, 2)

        @pl.when(seq == 0)
        def _():
            for c in copies(run_expert_ref[0], 0, 0):
                c.start()

        for c in copies(be_ref[i], j, slot):
            c.wait()
        more_runs = run + 1 < n_runs

        @pl.when(more_runs)
        def _():
            for c in copies(run_expert_ref[run + 1], j, 1 - slot):
                c.start()

        @pl.when(jnp.logical_and(jnp.logical_not(more_runs), j + 1 < pl.num_programs(0)))
        def _():
            for c in copies(run_expert_ref[0], j + 1, 1 - slot):
                c.start()

        for s, b in zip(slots, bf_copies):
            b[...] = s[slot].astype(BF16)


def _for_valid_rows(valid, tm, compute, zero_fill):
    half = tm // 2

    @pl.when(valid > half)
    def _():
        compute(tm)

    @pl.when(jnp.logical_and(valid > 0, valid <= half))
    def _():
        compute(half)
        zero_fill(half, tm)

    @pl.when(valid == 0)
    def _():
        zero_fill(0, tm)


def _moe_up_kernel(be_ref, valid_ref, run_ref, run_expert_ref, nruns_ref, x_ref, wg_hbm, wu_hbm,
                   bg_ref, bu_ref, h_ref, wg_slots, wu_slots, wg_bf, wu_bf, sem):
    i = pl.program_id(1)
    _weight_run_pipeline(be_ref, run_ref, run_expert_ref, nruns_ref, (wg_hbm, wu_hbm),
                         (wg_slots, wu_slots), (wg_bf, wu_bf), sem, h_ref.shape[1])

    def compute(rows):
        x = x_ref[0:rows, :]
        gate = jnp.minimum(_dot(x, wg_bf[...]) + bg_ref[0], SWIGLU_LIMIT)
        up = jnp.clip(_dot(x, wu_bf[...]) + bu_ref[0], -SWIGLU_LIMIT, SWIGLU_LIMIT)
        h_ref[0:rows, :] = ((up + 1.0) * (gate * _sigmoid(gate * SWIGLU_ALPHA))).astype(h_ref.dtype)

    def zero_fill(lo, hi):
        h_ref[lo:hi, :] = jnp.zeros((hi - lo, h_ref.shape[1]), h_ref.dtype)

    _for_valid_rows(valid_ref[i], x_ref.shape[0], compute, zero_fill)


def _moe_down_kernel(be_ref, valid_ref, run_ref, run_expert_ref, nruns_ref, h_ref, wd_hbm, bd_ref,
                     y_ref, wd_slots, wd_bf, slab_ref, sem):
    i = pl.program_id(1)
    tm, nt, _ = y_ref.shape
    _weight_run_pipeline(be_ref, run_ref, run_expert_ref, nruns_ref, (wd_hbm,), (wd_slots,),
                         (wd_bf,), sem, nt * LANES)

    def compute(rows):
        _store_slabs(slab_ref, 0, _dot(h_ref[0:rows, :], wd_bf[...]) + bd_ref[0])
        y_ref[0:rows] = slab_ref[0:rows * nt, :].reshape(rows, nt, LANES)

    def zero_fill(lo, hi):
        y_ref[lo:hi] = jnp.zeros((hi - lo, nt, LANES), y_ref.dtype)

    _for_valid_rows(valid_ref[i], tm, compute, zero_fill)


def _moe_up(xs, tables, w_gate, b_gate, w_up, b_up, tm, tn):
    n_pad, d = xs.shape
    ne, _, ff = w_gate.shape
    hbm = pl.BlockSpec(memory_space=pl.ANY)
    b_spec = pl.BlockSpec((1, 1, tn), lambda j, i, be, *_: (be[i], 0, j))
    return pl.pallas_call(
        _moe_up_kernel,
        grid_spec=pltpu.PrefetchScalarGridSpec(
            num_scalar_prefetch=len(tables),
            grid=(ff // tn, n_pad // tm),
            in_specs=[pl.BlockSpec((tm, d), lambda j, i, *_: (i, 0)), hbm, hbm, b_spec, b_spec],
            out_specs=pl.BlockSpec((tm, tn), lambda j, i, *_: (i, j)),
            scratch_shapes=[pltpu.VMEM((2, d, tn), F32), pltpu.VMEM((2, d, tn), F32),
                            pltpu.VMEM((d, tn), BF16), pltpu.VMEM((d, tn), BF16),
                            pltpu.SemaphoreType.DMA((2,))],
        ),
        out_shape=jax.ShapeDtypeStruct((n_pad, ff), BF16),
        compiler_params=_params("arbitrary", "arbitrary"),
    )(*tables, xs, w_gate, w_up, b_gate.reshape(ne, 1, ff), b_up.reshape(ne, 1, ff))


def _moe_down(hs, tables, w_down, b_down, tm, tn):
    n_pad, ff = hs.shape
    ne, _, d = w_down.shape
    nt = tn // LANES
    y = pl.pallas_call(
        _moe_down_kernel,
        grid_spec=pltpu.PrefetchScalarGridSpec(
            num_scalar_prefetch=len(tables),
            grid=(d // tn, n_pad // tm),
            in_specs=[pl.BlockSpec((tm, ff), lambda j, i, *_: (i, 0)),
                      pl.BlockSpec(memory_space=pl.ANY),
                      pl.BlockSpec((1, 1, tn), lambda j, i, be, *_: (be[i], 0, j))],
            out_specs=pl.BlockSpec((tm, nt, LANES), lambda j, i, *_: (i, j, 0)),
            scratch_shapes=[pltpu.VMEM((2, ff, tn), F32), pltpu.VMEM((ff, tn), BF16),
                            pltpu.VMEM((tm * nt, LANES), F32), pltpu.SemaphoreType.DMA((2,))],
        ),
        out_shape=jax.ShapeDtypeStruct((n_pad, d // LANES, LANES), F32),
        compiler_params=_params("arbitrary", "arbitrary"),
    )(*tables, hs, w_down, b_down.reshape(ne, 1, d))
    return y.reshape(n_pad * (d // LANES), LANES)


def _combine_ple_kernel(pos_ref, pos_next_ref, y_ref, w_ref, h_ref, p_ref, gffn_ref, gple_ref,
                        wgate_ref, wproj_ref, gpost_ref, o_ref, buf_ref, ff_ref, sem):
    tm, d = h_ref.shape
    n = d // LANES
    i = pl.program_id(0)

    def issue(idx_ref, slot):
        _issue_row_copies(TOP_K * tm, lambda r: idx_ref[0, 0, r], y_ref, buf_ref.at[slot],
                          sem.at[slot], n)

    for slot in range(2):
        @pl.when(i % 2 == slot)
        def _(slot=slot):
            if slot == 0:
                @pl.when(i == 0)
                def _():
                    issue(pos_ref, 0)

            @pl.when(i + 1 < pl.num_programs(0))
            def _():
                issue(pos_next_ref, 1 - slot)

            pltpu.make_async_copy(y_ref.at[pl.ds(0, TOP_K * tm * n)], buf_ref.at[slot],
                                  sem.at[slot]).wait()
            w = w_ref[...]
            ff = w[:, 0:1] * _load_slabs(buf_ref.at[slot], 0, tm, n)
            for kk in range(1, TOP_K):
                ff = ff + w[:, kk:kk + 1] * _load_slabs(buf_ref.at[slot], kk * tm * n, tm, n)
            ff_ref[...] = ff

    h = h_ref[...] + _rms(ff_ref[...], gffn_ref[...])
    gate = _sigmoid(_dot(_rms(h, gple_ref[...]).astype(BF16), wgate_ref[...]))
    emb = _dot(p_ref[...].astype(BF16), wproj_ref[...])
    o_ref[...] = h + _rms(gate * emb, gpost_ref[...])


def _combine_ple(y_rows, pos, top_w, h, p, ffn_gain, ple_gain, w_ple_gate, w_ple_proj,
                 ple_post_gain, tm):
    t, d = h.shape
    rows = lambda width: pl.BlockSpec((tm, width), lambda i: (i, 0))
    whole = lambda a: pl.BlockSpec(a.shape, lambda i: (0, 0), pipeline_mode=pl.Buffered(1))
    consts = (ffn_gain.reshape(1, d), ple_gain.reshape(1, d), w_ple_gate, w_ple_proj,
              ple_post_gain.reshape(1, d))
    steps = t // tm
    return pl.pallas_call(
        _combine_ple_kernel,
        grid=(steps,),
        in_specs=[
            pl.BlockSpec((1, 1, TOP_K * tm), lambda i: (i, 0, 0), memory_space=pltpu.SMEM),
            pl.BlockSpec((1, 1, TOP_K * tm), lambda i: (jnp.minimum(i + 1, steps - 1), 0, 0),
                         memory_space=pltpu.SMEM),
            pl.BlockSpec(memory_space=pl.ANY),
            rows(TOP_K), rows(d), rows(p.shape[1]),
        ] + [whole(a) for a in consts],
        out_specs=rows(d),
        out_shape=jax.ShapeDtypeStruct((t, d), F32),
        scratch_shapes=[pltpu.VMEM((2, TOP_K * tm * (d // LANES), LANES), F32),
                        pltpu.VMEM((tm, d), F32), pltpu.SemaphoreType.DMA((2,))],
        compiler_params=_params("arbitrary"),
    )(pos, pos, y_rows, top_w, h, p, *consts)


def _routing_tables(top_e, tm, tok_block):
    t = top_e.shape[0]
    n = t * TOP_K
    n_blocks = n // tm + N_EXPERTS
    e_flat = top_e.reshape(n)
    experts = jnp.arange(N_EXPERTS, dtype=jnp.int32)
    order = jnp.argsort(e_flat, stable=True).astype(jnp.int32)
    rank = jnp.argsort(order).astype(jnp.int32)
    onehot = e_flat[:, None] == experts[None, :]
    counts = jnp.sum(onehot, axis=0, dtype=jnp.int32)
    group_start = jnp.cumsum(counts) - counts
    padded = ((counts + tm - 1) // tm) * tm
    padded_end = jnp.cumsum(padded)
    padded_start = padded_end - padded
    shift = padded_start - group_start
    pos = (rank + jnp.sum(jnp.where(onehot, shift[None, :], 0), axis=1)).reshape(t, TOP_K)
    n_used = (padded_end[-1] // tm).astype(jnp.int32).reshape(1)
    offs = jnp.arange(n_blocks, dtype=jnp.int32) * tm
    block_e = jnp.sum(padded_end[None, :] <= offs[:, None], axis=1, dtype=jnp.int32)
    last_e = jnp.sum(padded_end <= (n_used[0] - 1) * tm, dtype=jnp.int32)
    block_e = jnp.minimum(jnp.where(offs < padded_end[-1], block_e, last_e), N_EXPERTS - 1)
    sel = block_e[:, None] == experts[None, :]
    pick = lambda table: jnp.sum(jnp.where(sel, table[None, :], 0), axis=1)
    in_group = offs - pick(padded_start)
    block_first = (pick(group_start) + in_group).astype(jnp.int32)
    block_valid = jnp.where(offs < padded_end[-1], jnp.clip(pick(counts) - in_group, 0, tm), 0)
    present = counts > 0
    run_of_expert = jnp.cumsum(present, dtype=jnp.int32) - 1
    n_runs = jnp.sum(present, dtype=jnp.int32).reshape(1)
    block_run = pick(run_of_expert).astype(jnp.int32)
    runs = jnp.arange(N_EXPERTS + 1, dtype=jnp.int32)
    run_expert = jnp.sum(jnp.where(present[None, :] & (run_of_expert[None, :] == runs[:, None]),
                                   experts[None, :], 0), axis=1, dtype=jnp.int32)
    pos_blocks = pos.reshape(t // tok_block, tok_block, TOP_K).transpose(0, 2, 1)
    moe_tables = (block_e, block_valid.astype(jnp.int32), block_run, run_expert, n_runs)
    return (order // TOP_K, block_first, pos_blocks.reshape(t // tok_block, 1, TOP_K * tok_block),
            n_used, moe_tables)


def _layer(h, p, pre_mix_norm, w_in, b_fgate, conv_w, conv_b, dt_bias, a_log, d_skip,
           attn_out_norm, ssm_out_norm, w_out, post_mix_norm, pre_ffn_norm, w_router, b_router,
           w_gate, b_gate, w_up, b_up, w_down, b_down, post_ffn_norm, ple_norm, w_ple_gate,
           w_ple_proj, ple_post_norm):
    bsz, seq, d = h.shape
    t = bsz * seq
    att_w = ATT_HEADS * ATT_HEAD_DIM
    heads = dt_bias.shape[0]
    ssm_w = heads * SSM_HEAD_DIM
    conv_ch = conv_w.shape[1]
    o_fg = 3 * att_w
    o_z = o_fg + ATT_HEADS
    o_xbc = o_z + ssm_w
    o_dt = o_xbc + conv_ch
    x2 = h.reshape(t, d)

    q_scale = LOG2E * ATT_HEAD_DIM ** -0.5
    w_qkv = jnp.concatenate([w_in[:, :att_w] * q_scale, w_in[:, att_w:o_fg]], axis=1).astype(BF16)
    w_z = w_in[:, o_z:o_xbc].astype(BF16)
    w_xbc = w_in[:, o_xbc:o_dt].astype(BF16)
    small_pad = LANES - ATT_HEADS - heads
    w_small = jnp.concatenate([w_in[:, o_fg:o_z], w_in[:, o_dt:], jnp.zeros((d, small_pad), F32)], axis=1)

    tm = min(1024, t)
    qkv = _rms_matmul(x2, pre_mix_norm, w_qkv, BF16, tm, 1024)
    zz = _rms_matmul(x2, pre_mix_norm, w_z, F32, tm, 1024)
    xbc = _rms_matmul(x2, pre_mix_norm, w_xbc, F32, tm, 1024)
    small = _rms_matmul(x2, pre_mix_norm, _split_weight(w_small), F32, tm, LANES, split=True)
    small = small.reshape(bsz, seq, LANES)

    fg_t = jnp.swapaxes(small[:, :, :ATT_HEADS], 1, 2)
    dt_raw = small[:, :, ATT_HEADS:ATT_HEADS + heads]
    c_t = _fg_cumsum(fg_t, b_fgate)
    att = _fox_attention(qkv.reshape(bsz, seq, 3 * att_w), c_t, min(512, seq))
    y = _ssd(xbc.reshape(bsz, seq, conv_ch), zz.reshape(bsz, seq, ssm_w), dt_raw,
             jnp.swapaxes(dt_raw, 1, 2), conv_w, conv_b, dt_bias, a_log, d_skip, ssm_out_norm)

    w_out_bf = w_out.astype(BF16)
    h1, hn, logits = _out_proj(att.reshape(t, att_w), y.reshape(t, ssm_w), x2, attn_out_norm,
                               w_out_bf[:att_w], w_out_bf[att_w:], post_mix_norm, pre_ffn_norm,
                               w_router, b_router, min(512, t))
    top_e, top_w = _router_topk(logits, min(2048, t))

    tok_block = min(256, t)
    tok_sorted, block_first, pos, n_used, moe_tables = _routing_tables(top_e, MOE_ROWS, tok_block)
    xs = _gather_rows(hn, tok_sorted, block_first, moe_tables[1], n_used, MOE_ROWS, d, BF16)
    hs = _moe_up(xs, moe_tables, w_gate, b_gate, w_up, b_up, MOE_ROWS, min(1024, w_gate.shape[2]))
    y_rows = _moe_down(hs, moe_tables, w_down, b_down, MOE_ROWS, min(1024, d))
    out = _combine_ple(y_rows, pos, top_w, h1, p.reshape(t, -1), post_ffn_norm, ple_norm,
                       w_ple_gate.astype(BF16), w_ple_proj.astype(BF16), ple_post_norm, tok_block)
    return out.reshape(bsz, seq, d)


def kernel(x, p, pre_mix_norm, w_in, b_fgate, conv_w, conv_b, dt_bias, a_log, d_skip, attn_out_norm, ssm_out_norm, w_out, post_mix_norm, pre_ffn_norm, w_router, b_router, w_gate, b_gate, w_up, b_up, w_down, b_down, post_ffn_norm, ple_norm, w_ple_gate, w_ple_proj, ple_post_norm):
    h = x
    for i in range(p.shape[0]):
        h = _layer(h, p[i], pre_mix_norm[i], w_in[i], b_fgate[i], conv_w[i], conv_b[i], dt_bias[i],
                   a_log[i], d_skip[i], attn_out_norm[i], ssm_out_norm[i], w_out[i],
                   post_mix_norm[i], pre_ffn_norm[i], w_router[i], b_router[i], w_gate[i], b_gate[i],
                   w_up[i], b_up[i], w_down[i], b_down[i], post_ffn_norm[i], ple_norm[i],
                   w_ple_gate[i], w_ple_proj[i], ple_post_norm[i])
    return h
```

```python
import functools

import jax
import jax.numpy as jnp
from jax import lax
from jax.experimental import pallas as pl
from jax.experimental.pallas import tpu as pltpu

F32 = jnp.float32
BF16 = jnp.bfloat16

ATT_HEADS = 8
ATT_HEAD_DIM = 128
SSM_HEAD_DIM = 64
SSM_GROUPS = 8
SSM_STATE = 128
SSM_CHUNK = 128
CONV_WIDTH = 4
N_EXPERTS = 32
TOP_K = 4
SWIGLU_LIMIT = 7.0
SWIGLU_ALPHA = 1.702
EPS = 1e-6
LOG2E = 1.4426950408889634

LANES = 128
SUBLANES = 8
VMEM_LIMIT_BYTES = 56 * 1024 * 1024

MOE_ROWS = 512


def _params(*semantics):
    return pltpu.CompilerParams(dimension_semantics=semantics, vmem_limit_bytes=VMEM_LIMIT_BYTES)


def _rms(x, gain):
    return x * lax.rsqrt(jnp.mean(x * x, axis=-1, keepdims=True) + EPS) * gain


def _sigmoid(x):
    return 1.0 / (1.0 + jnp.exp(-x))


def _softplus(x):
    return jnp.maximum(x, 0.0) + jnp.log1p(jnp.exp(-jnp.abs(x)))


def _split3(x):
    x1 = x.astype(BF16)
    r1 = x - x1.astype(F32)
    x2 = r1.astype(BF16)
    x3 = (r1 - x2.astype(F32)).astype(BF16)
    return x1, x2, x3


def _dot(a, b):
    return jnp.dot(a, b, preferred_element_type=F32)


def _dot_exact_rhs(x, m):
    x1, x2, x3 = _split3(x)
    return _dot(x1, m) + _dot(x2, m) + _dot(x3, m)


def _dot_exact_lhs(m, x):
    x1, x2, x3 = _split3(x)
    return _dot(m, x1) + _dot(m, x2) + _dot(m, x3)


def _split_weight(w):
    hi = w.astype(BF16)
    return jnp.concatenate([hi, (w - hi.astype(F32)).astype(BF16)], axis=1)


def _dot_split(x, w_cat):
    n = w_cat.shape[1] // 2
    x_hi = x.astype(BF16)
    x_lo = (x - x_hi.astype(F32)).astype(BF16)
    a = _dot(x_hi, w_cat)
    return a[:, :n] + a[:, n:] + _dot(x_lo, w_cat[:, :n])


def _store_slabs(ref, base, value):
    rows, width = value.shape
    n = width // LANES
    for c in range(n):
        ref[pl.ds(base + c, rows, stride=n), :] = value[:, c * LANES:(c + 1) * LANES]


PACKED_SLAB = 2 * LANES


def _pack_bf16_pairs(first, second):
    as_bits = lambda v: lax.bitcast_convert_type(v.astype(BF16).astype(F32), jnp.uint32)
    return as_bits(first) | (as_bits(second) >> 16)


def _unpack_bf16_pairs(u):
    return (lax.bitcast_convert_type(u & jnp.uint32(0xFFFF0000), F32),
            lax.bitcast_convert_type(u << 16, F32))


def _load_slabs(ref, base, rows, n):
    return jnp.concatenate([ref[pl.ds(base + c, rows, stride=n), :] for c in range(n)], axis=1)


def _rms_matmul_kernel(x_ref, g_ref, w_ref, o_ref, xn_ref):
    @pl.when(pl.program_id(1) == 0)
    def _():
        xn_ref[...] = _rms(x_ref[...], g_ref[...]).astype(xn_ref.dtype)

    if xn_ref.dtype == F32:
        acc = _dot_split(xn_ref[...], w_ref[...])
    else:
        acc = _dot(xn_ref[...], w_ref[...])
    o_ref[...] = acc.astype(o_ref.dtype)


def _rms_matmul(x, gain, w, out_dtype, tm, tn, split=False):
    m, k = x.shape
    n = w.shape[1] // 2 if split else w.shape[1]
    assert not split or n == tn
    return pl.pallas_call(
        _rms_matmul_kernel,
        grid=(m // tm, n // tn),
        in_specs=[
            pl.BlockSpec((tm, k), lambda i, j: (i, 0)),
            pl.BlockSpec((1, k), lambda i, j: (0, 0)),
            pl.BlockSpec((k, 2 * tn if split else tn), lambda i, j: (0, j)),
        ],
        out_specs=pl.BlockSpec((tm, tn), lambda i, j: (i, j)),
        out_shape=jax.ShapeDtypeStruct((m, n), out_dtype),
        scratch_shapes=[pltpu.VMEM((tm, k), F32 if split else BF16)],
        compiler_params=_params("parallel", "arbitrary"),
    )(x, gain.reshape(1, k), w)


def _fg_cumsum_kernel(fg_ref, b_ref, c_ref):
    seq = fg_ref.shape[2]
    row = lax.broadcasted_iota(jnp.int32, (LANES, LANES), 0)
    col = lax.broadcasted_iota(jnp.int32, (LANES, LANES), 1)
    triu = (row <= col).astype(BF16)
    carry = jnp.zeros((fg_ref.shape[1], 1), F32)
    for ci in range(seq // LANES):
        z = fg_ref[0, :, ci * LANES:(ci + 1) * LANES] + b_ref[...]
        ls = jnp.minimum(z, 0.0) - jnp.log1p(jnp.exp(-jnp.abs(z)))
        cs = _dot_exact_rhs(ls, triu) + carry
        c_ref[0, :, ci * LANES:(ci + 1) * LANES] = cs * LOG2E
        carry = cs[:, LANES - 1:LANES]


def _fg_cumsum(fg_t, b_fgate):
    bsz, heads, seq = fg_t.shape
    return pl.pallas_call(
        _fg_cumsum_kernel,
        grid=(bsz,),
        in_specs=[
            pl.BlockSpec((1, heads, seq), lambda b: (b, 0, 0)),
            pl.BlockSpec((heads, 1), lambda b: (0, 0)),
        ],
        out_specs=pl.BlockSpec((1, heads, seq), lambda b: (b, 0, 0)),
        out_shape=jax.ShapeDtypeStruct((bsz, heads, seq), F32),
        compiler_params=_params("parallel"),
    )(fg_t, b_fgate.reshape(heads, 1))


def _attn_kernel(q_ref, k_ref, v_ref, ck_ref, o_ref, *, blk):
    qi = pl.program_id(2)
    q = q_ref[0]

    def step(j, carry, masked):
        m, l, acc = carry
        start = pl.multiple_of(j * blk, blk)
        kj = k_ref[0, pl.ds(start, blk), :]
        vj = v_ref[0, pl.ds(start, blk), :]
        s = lax.dot_general(q, kj, (((1,), (1,)), ((), ())), preferred_element_type=F32)
        s = s - ck_ref[0, 0, j]
        if masked:
            row = lax.broadcasted_iota(jnp.int32, (blk, blk), 0)
            col = lax.broadcasted_iota(jnp.int32, (blk, blk), 1)
            s = jnp.where(col <= row, s, -jnp.inf)
        m_new = jnp.maximum(m, jnp.max(s, axis=-1, keepdims=True))
        alpha = jnp.exp2(m - m_new)
        p = jnp.exp2(s - m_new)
        l = alpha * l + jnp.sum(p, axis=-1, keepdims=True)
        acc = alpha * acc + _dot(p.astype(BF16), vj)
        return m_new, l, acc

    init = (jnp.full((blk, 1), -jnp.inf, F32), jnp.zeros((blk, 1), F32),
            jnp.zeros((blk, q.shape[-1]), F32))
    carry = lax.fori_loop(0, qi, lambda j, c: step(j, c, False), init)
    _, l, acc = step(qi, carry, True)
    o_ref[0] = (acc / l).astype(o_ref.dtype)


def _fox_attention(qkv, c_t, blk):
    bsz, seq, _ = qkv.shape
    h, dh = ATT_HEADS, ATT_HEAD_DIM
    ck = c_t.reshape(bsz, h, seq // blk, 1, blk)
    return pl.pallas_call(
        functools.partial(_attn_kernel, blk=blk),
        grid=(bsz, h, seq // blk),
        in_specs=[
            pl.BlockSpec((1, blk, dh), lambda b, hh, i: (b, i, hh)),
            pl.BlockSpec((1, seq, dh), lambda b, hh, i: (b, 0, h + hh)),
            pl.BlockSpec((1, seq, dh), lambda b, hh, i: (b, 0, 2 * h + hh)),
            pl.BlockSpec((1, 1, seq // blk, 1, blk), lambda b, hh, i: (b, hh, 0, 0, 0)),
        ],
        out_specs=pl.BlockSpec((1, blk, dh), lambda b, hh, i: (b, i, hh)),
        out_shape=jax.ShapeDtypeStruct((bsz, seq, h * dh), BF16),
        compiler_params=_params("parallel", "parallel", "arbitrary"),
    )(qkv, qkv, qkv, ck)


def _ssd_kernel(xbc_ref, z_ref, dt_ref, dtt_ref, cw_ref, cb_ref, dtb_ref, dtbt_ref, alog_ref,
                alogt_ref, dskip_ref, gain_ref, expand_ref, y_ref, ext_ref, act_ref, state_ref):
    chunk = SSM_CHUNK
    width = y_ref.shape[2]
    heads = dt_ref.shape[2]
    per_group = heads // SSM_GROUPS
    gw = per_group * SSM_HEAD_DIM
    nstate = SSM_STATE
    halo = SUBLANES
    c_idx = pl.program_id(1)

    @pl.when(c_idx == 0)
    def _():
        ext_ref[0:halo, :] = jnp.zeros((halo, ext_ref.shape[1]), F32)
        state_ref[...] = jnp.zeros(state_ref.shape, F32)

    ext_ref[halo:halo + chunk, :] = xbc_ref[0]
    conv_ch = ext_ref.shape[1]
    col_tile = 4 * LANES
    for ct in range(conv_ch // col_tile):
        sl = slice(ct * col_tile, (ct + 1) * col_tile)
        acc = cb_ref[:, sl] + cw_ref[CONV_WIDTH - 1:CONV_WIDTH, sl] * ext_ref[halo:halo + chunk, sl]
        for kk in range(CONV_WIDTH - 1):
            off = halo - (CONV_WIDTH - 1) + kk
            acc = acc + cw_ref[kk:kk + 1, sl] * ext_ref[off:off + chunk, sl]
        act_ref[:, sl] = acc * _sigmoid(acc)
    ext_ref[0:halo, :] = xbc_ref[0, chunk - halo:chunk, :]

    row = lax.broadcasted_iota(jnp.int32, (chunk, chunk), 0)
    col = lax.broadcasted_iota(jnp.int32, (chunk, chunk), 1)
    causal = row >= col
    tril = causal.astype(BF16)
    triu = (row <= col).astype(BF16)
    dt = _softplus(dt_ref[0] + dtb_ref[...])
    adt = dt * (-jnp.exp(alog_ref[...]))
    acs = _dot_exact_lhs(tril, adt)
    dtt = _softplus(dtt_ref[0] + dtbt_ref[...])
    acs_t = _dot_exact_rhs(dtt * (-jnp.exp(alogt_ref[...])), triu)

    expand = expand_ref[...]
    dt_x = _dot_exact_rhs(dt, expand)
    acs_x = _dot_exact_rhs(acs, expand)
    last_x = acs_x[chunk - 1:chunk, :]
    xs = act_ref[:, 0:width]
    xc = xs * dt_x
    xc_bf = xc.astype(BF16)
    xcd_bf = (xc * jnp.exp(last_x - acs_x)).astype(BF16)
    in_decay = jnp.exp(acs_x)
    chunk_decay = jnp.exp(last_x)
    lane = lax.broadcasted_iota(jnp.int32, (chunk, LANES), 1)
    low_half = lane < SSM_HEAD_DIM

    for g in range(SSM_GROUPS):
        gsl = slice(g * gw, (g + 1) * gw)
        b_g = act_ref[:, width + g * nstate:width + (g + 1) * nstate]
        c_g = act_ref[:, width + (SSM_GROUPS + g) * nstate:width + (SSM_GROUPS + g + 1) * nstate]
        c_bf = c_g.astype(BF16)
        cb = lax.dot_general(c_bf, b_g.astype(BF16), (((1,), (1,)), ((), ())),
                             preferred_element_type=F32)
        pieces = []
        for pair in range(per_group // 2):
            ms = []
            for r in (2 * pair, 2 * pair + 1):
                hd = g * per_group + r
                seg = acs[:, hd:hd + 1] - acs_t[hd:hd + 1, :]
                decay = jnp.exp(jnp.where(causal, seg, -jnp.inf))
                ms.append((cb * decay).astype(BF16))
            xpair = xc_bf[:, g * gw + pair * LANES:g * gw + (pair + 1) * LANES]
            zero = jnp.zeros_like(xpair)
            rhs = jnp.concatenate([jnp.where(low_half, xpair, zero),
                                   jnp.where(low_half, zero, xpair)], axis=0)
            pieces.append(_dot(jnp.concatenate(ms, axis=1), rhs))
        y_diag = jnp.concatenate(pieces, axis=1)
        state = state_ref[g]
        y_off = _dot(c_bf, state.astype(BF16)) * in_decay[:, gsl]
        y = y_diag + y_off + dskip_ref[:, gsl] * xs[:, gsl]
        zg = z_ref[0, :, gsl]
        y = y * (zg * _sigmoid(zg))
        y = y * lax.rsqrt(jnp.mean(y * y, axis=-1, keepdims=True) + EPS) * gain_ref[:, gsl]
        y_ref[0, :, gsl] = y.astype(y_ref.dtype)
        state_ref[g] = state * chunk_decay[:, gsl] + _dot(b_g.T.astype(BF16), xcd_bf[:, gsl])


def _ssd(xbc, z, dt_raw, dt_raw_t, conv_w, conv_b, dt_bias, a_log, d_skip, gain):
    bsz, seq, conv_ch = xbc.shape
    width = z.shape[2]
    heads = dt_raw.shape[2]
    chunk = SSM_CHUNK
    gw = width // SSM_GROUPS
    expand = (jnp.arange(width)[None, :] // SSM_HEAD_DIM == jnp.arange(heads)[:, None]).astype(BF16)
    dskip_x = jnp.repeat(d_skip, SSM_HEAD_DIM).reshape(1, width)
    const = lambda shape: pl.BlockSpec(shape, lambda b, c: (0,) * len(shape))
    return pl.pallas_call(
        _ssd_kernel,
        grid=(bsz, seq // chunk),
        in_specs=[
            pl.BlockSpec((1, chunk, conv_ch), lambda b, c: (b, c, 0)),
            pl.BlockSpec((1, chunk, width), lambda b, c: (b, c, 0)),
            pl.BlockSpec((1, chunk, heads), lambda b, c: (b, c, 0)),
            pl.BlockSpec((1, heads, chunk), lambda b, c: (b, 0, c)),
            const((CONV_WIDTH, conv_ch)),
            const((1, conv_ch)),
            const((1, heads)),
            const((heads, 1)),
            const((1, heads)),
            const((heads, 1)),
            const((1, width)),
            const((1, width)),
            const((heads, width)),
        ],
        out_specs=pl.BlockSpec((1, chunk, width), lambda b, c: (b, c, 0)),
        out_shape=jax.ShapeDtypeStruct((bsz, seq, width), BF16),
        scratch_shapes=[
            pltpu.VMEM((SUBLANES + chunk, conv_ch), F32),
            pltpu.VMEM((chunk, conv_ch), F32),
            pltpu.VMEM((SSM_GROUPS, SSM_STATE, gw), F32),
        ],
        compiler_params=_params("parallel", "arbitrary"),
    )(xbc, z, dt_raw, dt_raw_t, conv_w, conv_b.reshape(1, conv_ch), dt_bias.reshape(1, heads),
      dt_bias.reshape(heads, 1), a_log.reshape(1, heads), a_log.reshape(heads, 1), dskip_x,
      gain.reshape(1, width), expand)


def _out_proj_kernel(att_ref, y_ref, x_ref, ga_ref, wa_ref, wy_ref, gpost_ref, gpre_ref, wr_ref,
                     br_ref, h_ref, hn_ref, logit_ref):
    att_n = _rms(att_ref[...].astype(F32), ga_ref[...]).astype(BF16)
    mixed = _dot(att_n, wa_ref[...]) + _dot(y_ref[...], wy_ref[...])
    h = x_ref[...] + _rms(mixed, gpost_ref[...])
    h_ref[...] = h
    hn = _rms(h, gpre_ref[...])
    half = hn.shape[1] // 2
    _store_slabs(hn_ref, 0, _pack_bf16_pairs(hn[:, :half], hn[:, half:]))
    ne = logit_ref.shape[1]
    logit_ref[...] = _dot_split(hn, wr_ref[...])[:, :ne] + br_ref[...]


def _out_proj(att, y, x, attn_gain, w_att, w_y, post_gain, pre_ffn_gain, w_router, b_router, tm):
    t, d = x.shape
    ne = w_router.shape[1]
    rows = lambda width: pl.BlockSpec((tm, width), lambda i: (i, 0))
    whole = lambda a: pl.BlockSpec(a.shape, lambda i: (0, 0), pipeline_mode=pl.Buffered(1))
    w_router_cat = _split_weight(jnp.pad(w_router, ((0, 0), (0, LANES - ne))))
    args = (att, y, x, attn_gain.reshape(1, -1), w_att, w_y, post_gain.reshape(1, d),
            pre_ffn_gain.reshape(1, d), w_router_cat, b_router.reshape(1, ne))
    return pl.pallas_call(
        _out_proj_kernel,
        grid=(t // tm,),
        in_specs=[rows(att.shape[1]), rows(y.shape[1]), rows(d)] + [whole(a) for a in args[3:]],
        out_specs=[rows(d), pl.BlockSpec((tm * (d // PACKED_SLAB), LANES), lambda i: (i, 0)), rows(ne)],
        out_shape=[jax.ShapeDtypeStruct((t, d), F32),
                   jax.ShapeDtypeStruct((t * (d // PACKED_SLAB), LANES), jnp.uint32),
                   jax.ShapeDtypeStruct((t, ne), F32)],
        compiler_params=_params("parallel"),
    )(*args)


def _topk_kernel(logit_ref, e_ref, w_ref):
    l = logit_ref[...]
    ne = l.shape[1]
    lane = lax.broadcasted_iota(jnp.int32, l.shape, 1)
    slot = lax.broadcasted_iota(jnp.int32, e_ref.shape, 1)
    e_out = jnp.zeros(e_ref.shape, jnp.int32)
    p_out = jnp.zeros(w_ref.shape, F32)
    top = None
    for kk in range(TOP_K):
        m = jnp.max(l, axis=-1, keepdims=True)
        idx = jnp.min(jnp.where(l == m, lane, ne), axis=-1, keepdims=True)
        if top is None:
            top = m
        e_out = jnp.where(slot == kk, idx, e_out)
        p_out = jnp.where(slot == kk, jnp.exp(m - top), p_out)
        l = jnp.where(lane == idx, -jnp.inf, l)
    e_ref[...] = e_out
    w_ref[...] = p_out / jnp.sum(p_out, axis=-1, keepdims=True)


def _router_topk(logits, tm):
    t, ne = logits.shape
    return pl.pallas_call(
        _topk_kernel,
        grid=(t // tm,),
        in_specs=[pl.BlockSpec((tm, ne), lambda i: (i, 0))],
        out_specs=[pl.BlockSpec((tm, TOP_K), lambda i: (i, 0))] * 2,
        out_shape=[jax.ShapeDtypeStruct((t, TOP_K), jnp.int32),
                   jax.ShapeDtypeStruct((t, TOP_K), F32)],
        compiler_params=_params("parallel"),
    )(logits)


DMA_ISSUE_UNROLL = 8


def _issue_row_copies(count, row_index, src_ref, dst_ref, sem, n):
    trips = lax.shift_right_logical(count + (DMA_ISSUE_UNROLL - 1), DMA_ISSUE_UNROLL.bit_length() - 1)

    def body(g, carry):
        for u in range(DMA_ISSUE_UNROLL):
            r = g * DMA_ISSUE_UNROLL + u
            pltpu.make_async_copy(src_ref.at[pl.ds(row_index(r) * n, n)],
                                  dst_ref.at[pl.ds(r * n, n)], sem).start(priority=u % 2)
        return carry

    lax.fori_loop(0, trips, body, 0)
    return trips


def _gather_kernel(tok_ref, first_ref, valid_ref, used_ref, src_ref, o_ref, buf_ref, sem):
    tm = o_ref.shape[0]
    n = buf_ref.shape[1] // tm
    i = pl.program_id(0)
    used = used_ref[0]
    last = tok_ref.shape[0] - 1

    group = DMA_ISSUE_UNROLL * n

    def issue(block, slot):
        first = first_ref[block]
        valid = valid_ref[block]
        token = lambda r: jnp.where(r < valid, tok_ref[jnp.clip(first + r, 0, last)], 0)
        _issue_row_copies(valid, token, src_ref, buf_ref.at[slot], sem.at[slot], n)

    def wait_group(g, slot):
        pltpu.make_async_copy(src_ref.at[pl.ds(0, group)], buf_ref.at[slot, pl.ds(g * group, group)],
                              sem.at[slot]).wait()

    @pl.when(i == 0)
    def _():
        buf_ref[...] = jnp.zeros(buf_ref.shape, buf_ref.dtype)

    for slot in range(2):
        @pl.when(jnp.logical_and(i % 2 == slot, i < used))
        def _(slot=slot):
            if slot == 0:
                @pl.when(i == 0)
                def _():
                    issue(0, 0)

            @pl.when(i + 1 < used)
            def _():
                issue(i + 1, 1 - slot)

            trips = lax.shift_right_logical(valid_ref[i] + (DMA_ISSUE_UNROLL - 1),
                                            DMA_ISSUE_UNROLL.bit_length() - 1)
            lax.fori_loop(0, trips, lambda g, c: (wait_group(g, slot), c)[1], 0)
            first_half, second_half = _unpack_bf16_pairs(_load_slabs(buf_ref.at[slot], 0, tm, n))
            half = o_ref.shape[1] // 2
            o_ref[:, :half] = first_half.astype(o_ref.dtype)
            o_ref[:, half:] = second_half.astype(o_ref.dtype)

    @pl.when(i >= used)
    def _():
        o_ref[...] = jnp.zeros(o_ref.shape, o_ref.dtype)


def _gather_rows(src_slabs, tok_sorted, block_first, block_valid, n_used, tm, d, out_dtype):
    n = d // PACKED_SLAB
    n_blocks = block_first.shape[0]
    return pl.pallas_call(
        _gather_kernel,
        grid_spec=pltpu.PrefetchScalarGridSpec(
            num_scalar_prefetch=4,
            grid=(n_blocks,),
            in_specs=[pl.BlockSpec(memory_space=pl.ANY)],
            out_specs=pl.BlockSpec((tm, d), lambda i, *_: (i, 0)),
            scratch_shapes=[pltpu.VMEM((2, tm * n, LANES), src_slabs.dtype),
                            pltpu.SemaphoreType.DMA((2,))],
        ),
        out_shape=jax.ShapeDtypeStruct((n_blocks * tm, d), out_dtype),
        compiler_params=_params("arbitrary"),
    )(tok_sorted, block_first, block_valid, n_used, src_slabs)


def _expert_changed(be_ref, i):
    return jnp.logical_or(i == 0, be_ref[i] != be_ref[jnp.maximum(i - 1, 0)])


def _weight_run_pipeline(be_ref, run_ref, run_expert_ref, nruns_ref, weights, slots, bf_copies, sem, tn):
    j = pl.program_id(0)
    i = pl.program_id(1)
    n_runs = nruns_ref[0]

    def copies(expert, col, slot):
        return [pltpu.make_async_copy(w.at[expert, :, pl.ds(pl.multiple_of(col * tn, tn), tn)],
                                      s.at[slot], sem.at[slot]) for w, s in zip(weights, slots)]

    @pl.when(_expert_changed(be_ref, i))
    def _():
        run = run_ref[i]
        seq = j * n_runs + run
        slot = lax.rem(seq, 2)

        @pl.when(seq == 0)
        def _():
            for c in copies(run_expert_ref[0], 0, 0):
                c.start()

        for c in copies(be_ref[i], j, slot):
            c.wait()
        more_runs = run + 1 < n_runs

        @pl.when(more_runs)
        def _():
            for c in copies(run_expert_ref[run + 1], j, 1 - slot):
                c.start()

        @pl.when(jnp.logical_and(jnp.logical_not(more_runs), j + 1 < pl.num_programs(0)))
        def _():
            for c in copies(run_expert_ref[0], j + 1, 1 - slot):
                c.start()

        for s, b in zip(slots, bf_copies):
            b[...] = s[slot].astype(BF16)


def _for_valid_rows(valid, tm, compute, zero_fill):
    half = tm // 2

    @pl.when(valid > half)
    def _():
        compute(tm)

    @pl.when(jnp.logical_and(valid > 0, valid <= half))
    def _():
        compute(half)
        zero_fill(half, tm)

    @pl.when(valid == 0)
    def _():
        zero_fill(0, tm)


def _moe_up_kernel(be_ref, valid_ref, run_ref, run_expert_ref, nruns_ref, x_ref, wg_hbm, wu_hbm,
                   bg_ref, bu_ref, h_ref, wg_slots, wu_slots, wg_bf, wu_bf, sem):
    i = pl.program_id(1)
    _weight_run_pipeline(be_ref, run_ref, run_expert_ref, nruns_ref, (wg_hbm, wu_hbm),
                         (wg_slots, wu_slots), (wg_bf, wu_bf), sem, h_ref.shape[1])

    def compute(rows):
        x = x_ref[0:rows, :]
        gate = jnp.minimum(_dot(x, wg_bf[...]) + bg_ref[0], SWIGLU_LIMIT)
        up = jnp.clip(_dot(x, wu_bf[...]) + bu_ref[0], -SWIGLU_LIMIT, SWIGLU_LIMIT)
        h_ref[0:rows, :] = ((up + 1.0) * (gate * _sigmoid(gate * SWIGLU_ALPHA))).astype(h_ref.dtype)

    def zero_fill(lo, hi):
        h_ref[lo:hi, :] = jnp.zeros((hi - lo, h_ref.shape[1]), h_ref.dtype)

    _for_valid_rows(valid_ref[i], x_ref.shape[0], compute, zero_fill)


def _moe_down_kernel(be_ref, valid_ref, run_ref, run_expert_ref, nruns_ref, h_ref, wd_hbm, bd_ref,
                     y_ref, wd_slots, wd_bf, sem):
    i = pl.program_id(1)
    tm = h_ref.shape[0]
    d = wd_bf.shape[1]
    half = d // 2
    n = d // PACKED_SLAB
    _weight_run_pipeline(be_ref, run_ref, run_expert_ref, nruns_ref, (wd_hbm,), (wd_slots,),
                         (wd_bf,), sem, d)

    def compute(rows):
        h = h_ref[0:rows, :]
        first = _dot(h, wd_bf[:, :half]) + bd_ref[0, :, :half]
        second = _dot(h, wd_bf[:, half:]) + bd_ref[0, :, half:]
        _store_slabs(y_ref, 0, _pack_bf16_pairs(first, second))

    def zero_fill(lo, hi):
        y_ref[lo * n:hi * n, :] = jnp.zeros(((hi - lo) * n, LANES), y_ref.dtype)

    _for_valid_rows(valid_ref[i], tm, compute, zero_fill)


def _moe_up(xs, tables, w_gate, b_gate, w_up, b_up, tm, tn):
    n_pad, d = xs.shape
    ne, _, ff = w_gate.shape
    hbm = pl.BlockSpec(memory_space=pl.ANY)
    b_spec = pl.BlockSpec((1, 1, tn), lambda j, i, be, *_: (be[i], 0, j))
    return pl.pallas_call(
        _moe_up_kernel,
        grid_spec=pltpu.PrefetchScalarGridSpec(
            num_scalar_prefetch=len(tables),
            grid=(ff // tn, n_pad // tm),
            in_specs=[pl.BlockSpec((tm, d), lambda j, i, *_: (i, 0)), hbm, hbm, b_spec, b_spec],
            out_specs=pl.BlockSpec((tm, tn), lambda j, i, *_: (i, j)),
            scratch_shapes=[pltpu.VMEM((2, d, tn), F32), pltpu.VMEM((2, d, tn), F32),
                            pltpu.VMEM((d, tn), BF16), pltpu.VMEM((d, tn), BF16),
                            pltpu.SemaphoreType.DMA((2,))],
        ),
        out_shape=jax.ShapeDtypeStruct((n_pad, ff), BF16),
        compiler_params=_params("arbitrary", "arbitrary"),
    )(*tables, xs, w_gate, w_up, b_gate.reshape(ne, 1, ff), b_up.reshape(ne, 1, ff))


def _moe_down(hs, tables, w_down, b_down, tm):
    n_pad, ff = hs.shape
    ne, _, d = w_down.shape
    n = d // PACKED_SLAB
    return pl.pallas_call(
        _moe_down_kernel,
        grid_spec=pltpu.PrefetchScalarGridSpec(
            num_scalar_prefetch=len(tables),
            grid=(1, n_pad // tm),
            in_specs=[pl.BlockSpec((tm, ff), lambda j, i, *_: (i, 0)),
                      pl.BlockSpec(memory_space=pl.ANY),
                      pl.BlockSpec((1, 1, d), lambda j, i, be, *_: (be[i], 0, 0))],
            out_specs=pl.BlockSpec((tm * n, LANES), lambda j, i, *_: (i, 0)),
            scratch_shapes=[pltpu.VMEM((2, ff, d), F32), pltpu.VMEM((ff, d), BF16),
                            pltpu.SemaphoreType.DMA((2,))],
        ),
        out_shape=jax.ShapeDtypeStruct((n_pad * n, LANES), jnp.uint32),
        compiler_params=_params("arbitrary", "arbitrary"),
    )(*tables, hs, w_down, b_down.reshape(ne, 1, d))


def _combine_ple_kernel(pos_ref, pos_next_ref, y_ref, w_ref, h_ref, p_ref, gffn_ref, gple_ref,
                        wgate_ref, wproj_ref, gpost_ref, o_ref, buf_ref, ff_ref, sem):
    tm, d = h_ref.shape
    n = d // PACKED_SLAB
    half = d // 2
    i = pl.program_id(0)

    def issue(idx_ref, slot):
        _issue_row_copies(TOP_K * tm, lambda r: idx_ref[0, 0, r], y_ref, buf_ref.at[slot],
                          sem.at[slot], n)

    for slot in range(2):
        @pl.when(i % 2 == slot)
        def _(slot=slot):
            if slot == 0:
                @pl.when(i == 0)
                def _():
                    issue(pos_ref, 0)

            @pl.when(i + 1 < pl.num_programs(0))
            def _():
                issue(pos_next_ref, 1 - slot)

            pltpu.make_async_copy(y_ref.at[pl.ds(0, TOP_K * tm * n)], buf_ref.at[slot],
                                  sem.at[slot]).wait()
            w = w_ref[...]
            first = second = None
            for kk in range(TOP_K):
                a, b = _unpack_bf16_pairs(_load_slabs(buf_ref.at[slot], kk * tm * n, tm, n))
                wk = w[:, kk:kk + 1]
                first = wk * a if first is None else first + wk * a
                second = wk * b if second is None else second + wk * b
            ff_ref[:, :half] = first
            ff_ref[:, half:] = second

    h = h_ref[...] + _rms(ff_ref[...], gffn_ref[...])
    gate = _sigmoid(_dot(_rms(h, gple_ref[...]).astype(BF16), wgate_ref[...]))
    emb = _dot(p_ref[...].astype(BF16), wproj_ref[...])
    o_ref[...] = h + _rms(gate * emb, gpost_ref[...])


def _combine_ple(y_rows, pos, top_w, h, p, ffn_gain, ple_gain, w_ple_gate, w_ple_proj,
                 ple_post_gain, tm):
    t, d = h.shape
    rows = lambda width: pl.BlockSpec((tm, width), lambda i: (i, 0))
    whole = lambda a: pl.BlockSpec(a.shape, lambda i: (0, 0), pipeline_mode=pl.Buffered(1))
    consts = (ffn_gain.reshape(1, d), ple_gain.reshape(1, d), w_ple_gate, w_ple_proj,
              ple_post_gain.reshape(1, d))
    steps = t // tm
    return pl.pallas_call(
        _combine_ple_kernel,
        grid=(steps,),
        in_specs=[
            pl.BlockSpec((1, 1, TOP_K * tm), lambda i: (i, 0, 0), memory_space=pltpu.SMEM),
            pl.BlockSpec((1, 1, TOP_K * tm), lambda i: (jnp.minimum(i + 1, steps - 1), 0, 0),
                         memory_space=pltpu.SMEM),
            pl.BlockSpec(memory_space=pl.ANY),
            rows(TOP_K), rows(d), rows(p.shape[1]),
        ] + [whole(a) for a in consts],
        out_specs=rows(d),
        out_shape=jax.ShapeDtypeStruct((t, d), F32),
        scratch_shapes=[pltpu.VMEM((2, TOP_K * tm * (d // PACKED_SLAB), LANES), jnp.uint32),
                        pltpu.VMEM((tm, d), F32), pltpu.SemaphoreType.DMA((2,))],
        compiler_params=_params("arbitrary"),
    )(pos, pos, y_rows, top_w, h, p, *consts)


def _routing_tables(top_e, tm, tok_block):
    t = top_e.shape[0]
    n = t * TOP_K
    n_blocks = n // tm + N_EXPERTS
    e_flat = top_e.reshape(n)
    experts = jnp.arange(N_EXPERTS, dtype=jnp.int32)
    order = jnp.argsort(e_flat, stable=True).astype(jnp.int32)
    rank = jnp.argsort(order).astype(jnp.int32)
    onehot = e_flat[:, None] == experts[None, :]
    counts = jnp.sum(onehot, axis=0, dtype=jnp.int32)
    group_start = jnp.cumsum(counts) - counts
    padded = ((counts + tm - 1) // tm) * tm
    padded_end = jnp.cumsum(padded)
    padded_start = padded_end - padded
    shift = padded_start - group_start
    pos = (rank + jnp.sum(jnp.where(onehot, shift[None, :], 0), axis=1)).reshape(t, TOP_K)
    n_used = (padded_end[-1] // tm).astype(jnp.int32).reshape(1)
    offs = jnp.arange(n_blocks, dtype=jnp.int32) * tm
    block_e = jnp.sum(padded_end[None, :] <= offs[:, None], axis=1, dtype=jnp.int32)
    last_e = jnp.sum(padded_end <= (n_used[0] - 1) * tm, dtype=jnp.int32)
    block_e = jnp.minimum(jnp.where(offs < padded_end[-1], block_e, last_e), N_EXPERTS - 1)
    sel = block_e[:, None] == experts[None, :]
    pick = lambda table: jnp.sum(jnp.where(sel, table[None, :], 0), axis=1)
    in_group = offs - pick(padded_start)
    block_first = (pick(group_start) + in_group).astype(jnp.int32)
    block_valid = jnp.where(offs < padded_end[-1], jnp.clip(pick(counts) - in_group, 0, tm), 0)
    present = counts > 0
    run_of_expert = jnp.cumsum(present, dtype=jnp.int32) - 1
    n_runs = jnp.sum(present, dtype=jnp.int32).reshape(1)
    block_run = pick(run_of_expert).astype(jnp.int32)
    runs = jnp.arange(N_EXPERTS + 1, dtype=jnp.int32)
    run_expert = jnp.sum(jnp.where(present[None, :] & (run_of_expert[None, :] == runs[:, None]),
                                   experts[None, :], 0), axis=1, dtype=jnp.int32)
    pos_blocks = pos.reshape(t // tok_block, tok_block, TOP_K).transpose(0, 2, 1)
    moe_tables = (block_e, block_valid.astype(jnp.int32), block_run, run_expert, n_runs)
    return (order // TOP_K, block_first, pos_blocks.reshape(t // tok_block, 1, TOP_K * tok_block),
            n_used, moe_tables)


def _layer(h, p, pre_mix_norm, w_in, b_fgate, conv_w, conv_b, dt_bias, a_log, d_skip,
           attn_out_norm, ssm_out_norm, w_out, post_mix_norm, pre_ffn_norm, w_router, b_router,
           w_gate, b_gate, w_up, b_up, w_down, b_down, post_ffn_norm, ple_norm, w_ple_gate,
           w_ple_proj, ple_post_norm):
    bsz, seq, d = h.shape
    t = bsz * seq
    att_w = ATT_HEADS * ATT_HEAD_DIM
    heads = dt_bias.shape[0]
    ssm_w = heads * SSM_HEAD_DIM
    conv_ch = conv_w.shape[1]
    o_fg = 3 * att_w
    o_z = o_fg + ATT_HEADS
    o_xbc = o_z + ssm_w
    o_dt = o_xbc + conv_ch
    x2 = h.reshape(t, d)

    q_scale = LOG2E * ATT_HEAD_DIM ** -0.5
    w_qkv = jnp.concatenate([w_in[:, :att_w] * q_scale, w_in[:, att_w:o_fg]], axis=1).astype(BF16)
    w_z = w_in[:, o_z:o_xbc].astype(BF16)
    w_xbc = w_in[:, o_xbc:o_dt].astype(BF16)
    small_pad = LANES - ATT_HEADS - heads
    w_small = jnp.concatenate([w_in[:, o_fg:o_z], w_in[:, o_dt:], jnp.zeros((d, small_pad), F32)], axis=1)

    tm = min(1024, t)
    qkv = _rms_matmul(x2, pre_mix_norm, w_qkv, BF16, tm, 1024)
    zz = _rms_matmul(x2, pre_mix_norm, w_z, F32, tm, 1024)
    xbc = _rms_matmul(x2, pre_mix_norm, w_xbc, F32, tm, 1024)
    small = _rms_matmul(x2, pre_mix_norm, _split_weight(w_small), F32, tm, LANES, split=True)
    small = small.reshape(bsz, seq, LANES)

    fg_t = jnp.swapaxes(small[:, :, :ATT_HEADS], 1, 2)
    dt_raw = small[:, :, ATT_HEADS:ATT_HEADS + heads]
    c_t = _fg_cumsum(fg_t, b_fgate)
    att = _fox_attention(qkv.reshape(bsz, seq, 3 * att_w), c_t, min(512, seq))
    y = _ssd(xbc.reshape(bsz, seq, conv_ch), zz.reshape(bsz, seq, ssm_w), dt_raw,
             jnp.swapaxes(dt_raw, 1, 2), conv_w, conv_b, dt_bias, a_log, d_skip, ssm_out_norm)

    w_out_bf = w_out.astype(BF16)
    h1, hn, logits = _out_proj(att.reshape(t, att_w), y.reshape(t, ssm_w), x2, attn_out_norm,
                               w_out_bf[:att_w], w_out_bf[att_w:], post_mix_norm, pre_ffn_norm,
                               w_router, b_router, min(512, t))
    top_e, top_w = _router_topk(logits, min(2048, t))

    tok_block = min(256, t)
    tok_sorted, block_first, pos, n_used, moe_tables = _routing_tables(top_e, MOE_ROWS, tok_block)
    xs = _gather_rows(hn, tok_sorted, block_first, moe_tables[1], n_used, MOE_ROWS, d, BF16)
    hs = _moe_up(xs, moe_tables, w_gate, b_gate, w_up, b_up, MOE_ROWS, min(1024, w_gate.shape[2]))
    y_rows = _moe_down(hs, moe_tables, w_down, b_down, MOE_ROWS)
    out = _combine_ple(y_rows, pos, top_w, h1, p.reshape(t, -1), post_ffn_norm, ple_norm,
                       w_ple_gate.astype(BF16), w_ple_proj.astype(BF16), ple_post_norm, tok_block)
    return out.reshape(bsz, seq, d)


def kernel(x, p, pre_mix_norm, w_in, b_fgate, conv_w, conv_b, dt_bias, a_log, d_skip, attn_out_norm, ssm_out_norm, w_out, post_mix_norm, pre_ffn_norm, w_router, b_router, w_gate, b_gate, w_up, b_up, w_down, b_down, post_ffn_norm, ple_norm, w_ple_gate, w_ple_proj, ple_post_norm):
    h = x
    for i in range(p.shape[0]):
        h = _layer(h, p[i], pre_mix_norm[i], w_in[i], b_fgate[i], conv_w[i], conv_b[i], dt_bias[i],
                   a_log[i], d_skip[i], attn_out_norm[i], ssm_out_norm[i], w_out[i],
                   post_mix_norm[i], pre_ffn_norm[i], w_router[i], b_router[i], w_gate[i], b_gate[i],
                   w_up[i], b_up[i], w_down[i], b_down[i], post_ffn_norm[i], ple_norm[i],
                   w_ple_gate[i], w_ple_proj[i], ple_post_norm[i])
    return h
```

```python
import functools

import jax
import jax.numpy as jnp
from jax import lax
from jax.experimental import pallas as pl
from jax.experimental.pallas import tpu as pltpu

F32 = jnp.float32
BF16 = jnp.bfloat16

ATT_HEADS = 8
ATT_HEAD_DIM = 128
SSM_HEAD_DIM = 64
SSM_GROUPS = 8
SSM_STATE = 128
SSM_CHUNK = 128
CONV_WIDTH = 4
N_EXPERTS = 32
TOP_K = 4
SWIGLU_LIMIT = 7.0
SWIGLU_ALPHA = 1.702
EPS = 1e-6
LOG2E = 1.4426950408889634

LANES = 128
SUBLANES = 8
VMEM_LIMIT_BYTES = 56 * 1024 * 1024

MOE_ROWS = 512


def _params(*semantics):
    return pltpu.CompilerParams(dimension_semantics=semantics, vmem_limit_bytes=VMEM_LIMIT_BYTES)


def _rms(x, gain):
    return x * lax.rsqrt(jnp.mean(x * x, axis=-1, keepdims=True) + EPS) * gain


def _sigmoid(x):
    return 1.0 / (1.0 + jnp.exp(-x))


def _softplus(x):
    return jnp.maximum(x, 0.0) + jnp.log1p(jnp.exp(-jnp.abs(x)))


def _split3(x):
    x1 = x.astype(BF16)
    r1 = x - x1.astype(F32)
    x2 = r1.astype(BF16)
    x3 = (r1 - x2.astype(F32)).astype(BF16)
    return x1, x2, x3


def _dot(a, b):
    return jnp.dot(a, b, preferred_element_type=F32)


def _dot_exact_rhs(x, m):
    x1, x2, x3 = _split3(x)
    return _dot(x1, m) + _dot(x2, m) + _dot(x3, m)


def _dot_exact_lhs(m, x):
    x1, x2, x3 = _split3(x)
    return _dot(m, x1) + _dot(m, x2) + _dot(m, x3)


def _split_weight(w):
    hi = w.astype(BF16)
    return jnp.concatenate([hi, (w - hi.astype(F32)).astype(BF16)], axis=1)


def _dot_split(x, w_cat):
    n = w_cat.shape[1] // 2
    x_hi = x.astype(BF16)
    x_lo = (x - x_hi.astype(F32)).astype(BF16)
    a = _dot(x_hi, w_cat)
    return a[:, :n] + a[:, n:] + _dot(x_lo, w_cat[:, :n])


def _store_slabs(ref, base, value):
    rows, width = value.shape
    n = width // LANES
    for c in range(n):
        ref[pl.ds(base + c, rows, stride=n), :] = value[:, c * LANES:(c + 1) * LANES]


PACKED_SLAB = 2 * LANES


def _pack_bf16_pairs(first, second):
    as_bits = lambda v: lax.bitcast_convert_type(v.astype(BF16).astype(F32), jnp.uint32)
    return as_bits(first) | (as_bits(second) >> 16)


def _unpack_bf16_pairs(u):
    return (lax.bitcast_convert_type(u & jnp.uint32(0xFFFF0000), F32),
            lax.bitcast_convert_type(u << 16, F32))


def _load_slabs(ref, base, rows, n):
    return jnp.concatenate([ref[pl.ds(base + c, rows, stride=n), :] for c in range(n)], axis=1)


def _rms_matmul_kernel(x_ref, g_ref, w_ref, o_ref, xn_ref):
    @pl.when(pl.program_id(1) == 0)
    def _():
        xn_ref[...] = _rms(x_ref[...], g_ref[...]).astype(xn_ref.dtype)

    if xn_ref.dtype == F32:
        acc = _dot_split(xn_ref[...], w_ref[...])
    else:
        acc = _dot(xn_ref[...], w_ref[...])
    o_ref[...] = acc.astype(o_ref.dtype)


def _rms_matmul(x, gain, w, out_dtype, tm, tn, split=False):
    m, k = x.shape
    n = w.shape[1] // 2 if split else w.shape[1]
    assert not split or n == tn
    return pl.pallas_call(
        _rms_matmul_kernel,
        grid=(m // tm, n // tn),
        in_specs=[
            pl.BlockSpec((tm, k), lambda i, j: (i, 0)),
            pl.BlockSpec((1, k), lambda i, j: (0, 0)),
            pl.BlockSpec((k, 2 * tn if split else tn), lambda i, j: (0, j)),
        ],
        out_specs=pl.BlockSpec((tm, tn), lambda i, j: (i, j)),
        out_shape=jax.ShapeDtypeStruct((m, n), out_dtype),
        scratch_shapes=[pltpu.VMEM((tm, k), F32 if split else BF16)],
        compiler_params=_params("parallel", "arbitrary"),
    )(x, gain.reshape(1, k), w)


def _fg_cumsum_kernel(fg_ref, b_ref, c_ref):
    seq = fg_ref.shape[2]
    row = lax.broadcasted_iota(jnp.int32, (LANES, LANES), 0)
    col = lax.broadcasted_iota(jnp.int32, (LANES, LANES), 1)
    triu = (row <= col).astype(BF16)
    carry = jnp.zeros((fg_ref.shape[1], 1), F32)
    for ci in range(seq // LANES):
        z = fg_ref[0, :, ci * LANES:(ci + 1) * LANES] + b_ref[...]
        ls = jnp.minimum(z, 0.0) - jnp.log1p(jnp.exp(-jnp.abs(z)))
        cs = _dot_exact_rhs(ls, triu) + carry
        c_ref[0, :, ci * LANES:(ci + 1) * LANES] = cs * LOG2E
        carry = cs[:, LANES - 1:LANES]


def _fg_cumsum(fg_t, b_fgate):
    bsz, heads, seq = fg_t.shape
    return pl.pallas_call(
        _fg_cumsum_kernel,
        grid=(bsz,),
        in_specs=[
            pl.BlockSpec((1, heads, seq), lambda b: (b, 0, 0)),
            pl.BlockSpec((heads, 1), lambda b: (0, 0)),
        ],
        out_specs=pl.BlockSpec((1, heads, seq), lambda b: (b, 0, 0)),
        out_shape=jax.ShapeDtypeStruct((bsz, heads, seq), F32),
        compiler_params=_params("parallel"),
    )(fg_t, b_fgate.reshape(heads, 1))


def _attn_kernel(q_ref, k_ref, v_ref, ck_ref, o_ref, *, blk):
    qi = pl.program_id(2)
    q = q_ref[0]

    def step(j, carry, masked):
        m, l, acc = carry
        start = pl.multiple_of(j * blk, blk)
        kj = k_ref[0, pl.ds(start, blk), :]
        vj = v_ref[0, pl.ds(start, blk), :]
        s = lax.dot_general(q, kj, (((1,), (1,)), ((), ())), preferred_element_type=F32)
        s = s - ck_ref[0, 0, j]
        if masked:
            row = lax.broadcasted_iota(jnp.int32, (blk, blk), 0)
            col = lax.broadcasted_iota(jnp.int32, (blk, blk), 1)
            s = jnp.where(col <= row, s, -jnp.inf)
        m_new = jnp.maximum(m, jnp.max(s, axis=-1, keepdims=True))
        alpha = jnp.exp2(m - m_new)
        p = jnp.exp2(s - m_new)
        l = alpha * l + jnp.sum(p, axis=-1, keepdims=True)
        acc = alpha * acc + _dot(p.astype(BF16), vj)
        return m_new, l, acc

    init = (jnp.full((blk, 1), -jnp.inf, F32), jnp.zeros((blk, 1), F32),
            jnp.zeros((blk, q.shape[-1]), F32))
    carry = lax.fori_loop(0, qi, lambda j, c: step(j, c, False), init)
    _, l, acc = step(qi, carry, True)
    o_ref[0] = (acc / l).astype(o_ref.dtype)


def _fox_attention(qkv, c_t, blk):
    bsz, seq, _ = qkv.shape
    h, dh = ATT_HEADS, ATT_HEAD_DIM
    ck = c_t.reshape(bsz, h, seq // blk, 1, blk)
    return pl.pallas_call(
        functools.partial(_attn_kernel, blk=blk),
        grid=(bsz, h, seq // blk),
        in_specs=[
            pl.BlockSpec((1, blk, dh), lambda b, hh, i: (b, i, hh)),
            pl.BlockSpec((1, seq, dh), lambda b, hh, i: (b, 0, h + hh)),
            pl.BlockSpec((1, seq, dh), lambda b, hh, i: (b, 0, 2 * h + hh)),
            pl.BlockSpec((1, 1, seq // blk, 1, blk), lambda b, hh, i: (b, hh, 0, 0, 0)),
        ],
        out_specs=pl.BlockSpec((1, blk, dh), lambda b, hh, i: (b, i, hh)),
        out_shape=jax.ShapeDtypeStruct((bsz, seq, h * dh), BF16),
        compiler_params=_params("parallel", "parallel", "arbitrary"),
    )(qkv, qkv, qkv, ck)


def _ssd_kernel(xbc_ref, z_ref, dt_ref, dtt_ref, cw_ref, cb_ref, dtb_ref, dtbt_ref, alog_ref,
                alogt_ref, dskip_ref, gain_ref, expand_ref, y_ref, ext_ref, act_ref, state_ref):
    chunk = SSM_CHUNK
    width = y_ref.shape[2]
    heads = dt_ref.shape[2]
    per_group = heads // SSM_GROUPS
    gw = per_group * SSM_HEAD_DIM
    nstate = SSM_STATE
    halo = SUBLANES
    c_idx = pl.program_id(1)

    @pl.when(c_idx == 0)
    def _():
        ext_ref[0:halo, :] = jnp.zeros((halo, ext_ref.shape[1]), F32)
        state_ref[...] = jnp.zeros(state_ref.shape, F32)

    ext_ref[halo:halo + chunk, :] = xbc_ref[0]
    conv_ch = ext_ref.shape[1]
    col_tile = 4 * LANES
    for ct in range(conv_ch // col_tile):
        sl = slice(ct * col_tile, (ct + 1) * col_tile)
        acc = cb_ref[:, sl] + cw_ref[CONV_WIDTH - 1:CONV_WIDTH, sl] * ext_ref[halo:halo + chunk, sl]
        for kk in range(CONV_WIDTH - 1):
            off = halo - (CONV_WIDTH - 1) + kk
            acc = acc + cw_ref[kk:kk + 1, sl] * ext_ref[off:off + chunk, sl]
        act_ref[:, sl] = acc * _sigmoid(acc)
    ext_ref[0:halo, :] = xbc_ref[0, chunk - halo:chunk, :]

    row = lax.broadcasted_iota(jnp.int32, (chunk, chunk), 0)
    col = lax.broadcasted_iota(jnp.int32, (chunk, chunk), 1)
    causal = row >= col
    tril = causal.astype(BF16)
    triu = (row <= col).astype(BF16)
    dt = _softplus(dt_ref[0] + dtb_ref[...])
    adt = dt * (-jnp.exp(alog_ref[...]))
    acs = _dot_exact_lhs(tril, adt)
    dtt = _softplus(dtt_ref[0] + dtbt_ref[...])
    acs_t = _dot_exact_rhs(dtt * (-jnp.exp(alogt_ref[...])), triu)

    expand = expand_ref[...]
    dt_x = _dot_exact_rhs(dt, expand)
    acs_x = _dot_exact_rhs(acs, expand)
    last_x = acs_x[chunk - 1:chunk, :]
    xs = act_ref[:, 0:width]
    xc = xs * dt_x
    xc_bf = xc.astype(BF16)
    xcd_bf = (xc * jnp.exp(last_x - acs_x)).astype(BF16)
    in_decay = jnp.exp(acs_x)
    chunk_decay = jnp.exp(last_x)
    lane = lax.broadcasted_iota(jnp.int32, (chunk, LANES), 1)
    low_half = lane < SSM_HEAD_DIM

    for g in range(SSM_GROUPS):
        gsl = slice(g * gw, (g + 1) * gw)
        b_g = act_ref[:, width + g * nstate:width + (g + 1) * nstate]
        c_g = act_ref[:, width + (SSM_GROUPS + g) * nstate:width + (SSM_GROUPS + g + 1) * nstate]
        c_bf = c_g.astype(BF16)
        cb = lax.dot_general(c_bf, b_g.astype(BF16), (((1,), (1,)), ((), ())),
                             preferred_element_type=F32)
        pieces = []
        for pair in range(per_group // 2):
            ms = []
            for r in (2 * pair, 2 * pair + 1):
                hd = g * per_group + r
                seg = acs[:, hd:hd + 1] - acs_t[hd:hd + 1, :]
                decay = jnp.exp(jnp.where(causal, seg, -jnp.inf))
                ms.append((cb * decay).astype(BF16))
            xpair = xc_bf[:, g * gw + pair * LANES:g * gw + (pair + 1) * LANES]
            zero = jnp.zeros_like(xpair)
            rhs = jnp.concatenate([jnp.where(low_half, xpair, zero),
                                   jnp.where(low_half, zero, xpair)], axis=0)
            pieces.append(_dot(jnp.concatenate(ms, axis=1), rhs))
        y_diag = jnp.concatenate(pieces, axis=1)
        state = state_ref[g]
        y_off = _dot(c_bf, state.astype(BF16)) * in_decay[:, gsl]
        y = y_diag + y_off + dskip_ref[:, gsl] * xs[:, gsl]
        zg = z_ref[0, :, gsl]
        y = y * (zg * _sigmoid(zg))
        y = y * lax.rsqrt(jnp.mean(y * y, axis=-1, keepdims=True) + EPS) * gain_ref[:, gsl]
        y_ref[0, :, gsl] = y.astype(y_ref.dtype)
        state_ref[g] = state * chunk_decay[:, gsl] + _dot(b_g.T.astype(BF16), xcd_bf[:, gsl])


def _ssd(xbc, z, dt_raw, dt_raw_t, conv_w, conv_b, dt_bias, a_log, d_skip, gain):
    bsz, seq, conv_ch = xbc.shape
    width = z.shape[2]
    heads = dt_raw.shape[2]
    chunk = SSM_CHUNK
    gw = width // SSM_GROUPS
    expand = (jnp.arange(width)[None, :] // SSM_HEAD_DIM == jnp.arange(heads)[:, None]).astype(BF16)
    dskip_x = jnp.repeat(d_skip, SSM_HEAD_DIM).reshape(1, width)
    const = lambda shape: pl.BlockSpec(shape, lambda b, c: (0,) * len(shape))
    return pl.pallas_call(
        _ssd_kernel,
        grid=(bsz, seq // chunk),
        in_specs=[
            pl.BlockSpec((1, chunk, conv_ch), lambda b, c: (b, c, 0)),
            pl.BlockSpec((1, chunk, width), lambda b, c: (b, c, 0)),
            pl.BlockSpec((1, chunk, heads), lambda b, c: (b, c, 0)),
            pl.BlockSpec((1, heads, chunk), lambda b, c: (b, 0, c)),
            const((CONV_WIDTH, conv_ch)),
            const((1, conv_ch)),
            const((1, heads)),
            const((heads, 1)),
            const((1, heads)),
            const((heads, 1)),
            const((1, width)),
            const((1, width)),
            const((heads, width)),
        ],
        out_specs=pl.BlockSpec((1, chunk, width), lambda b, c: (b, c, 0)),
        out_shape=jax.ShapeDtypeStruct((bsz, seq, width), BF16),
        scratch_shapes=[
            pltpu.VMEM((SUBLANES + chunk, conv_ch), F32),
            pltpu.VMEM((chunk, conv_ch), F32),
            pltpu.VMEM((SSM_GROUPS, SSM_STATE, gw), F32),
        ],
        compiler_params=_params("parallel", "arbitrary"),
    )(xbc, z, dt_raw, dt_raw_t, conv_w, conv_b.reshape(1, conv_ch), dt_bias.reshape(1, heads),
      dt_bias.reshape(heads, 1), a_log.reshape(1, heads), a_log.reshape(heads, 1), dskip_x,
      gain.reshape(1, width), expand)


def _top_k_softmax(l):
    rows, ne = l.shape
    lane = lax.broadcasted_iota(jnp.int32, l.shape, 1)
    slot = lax.broadcasted_iota(jnp.int32, (rows, TOP_K), 1)
    e_out = jnp.zeros((rows, TOP_K), jnp.int32)
    p_out = jnp.zeros((rows, TOP_K), F32)
    top = None
    for kk in range(TOP_K):
        m = jnp.max(l, axis=-1, keepdims=True)
        idx = jnp.min(jnp.where(l == m, lane, ne), axis=-1, keepdims=True)
        if top is None:
            top = m
        e_out = jnp.where(slot == kk, idx, e_out)
        p_out = jnp.where(slot == kk, jnp.exp(m - top), p_out)
        l = jnp.where(lane == idx, -jnp.inf, l)
    return e_out, p_out / jnp.sum(p_out, axis=-1, keepdims=True)


def _out_proj_kernel(att_ref, y_ref, x_ref, ga_ref, wa_ref, wy_ref, gpost_ref, gpre_ref, wr_ref,
                     br_ref, h_ref, hn_ref, e_ref, w_ref):
    att_n = _rms(att_ref[...].astype(F32), ga_ref[...]).astype(BF16)
    mixed = _dot(att_n, wa_ref[...]) + _dot(y_ref[...], wy_ref[...])
    h = x_ref[...] + _rms(mixed, gpost_ref[...])
    h_ref[...] = h
    hn = _rms(h, gpre_ref[...])
    half = hn.shape[1] // 2
    _store_slabs(hn_ref, 0, _pack_bf16_pairs(hn[:, :half], hn[:, half:]))
    ne = br_ref.shape[1]
    e_ref[...], w_ref[...] = _top_k_softmax(_dot_split(hn, wr_ref[...])[:, :ne] + br_ref[...])


def _out_proj(att, y, x, attn_gain, w_att, w_y, post_gain, pre_ffn_gain, w_router, b_router, tm):
    t, d = x.shape
    ne = w_router.shape[1]
    rows = lambda width: pl.BlockSpec((tm, width), lambda i: (i, 0))
    whole = lambda a: pl.BlockSpec(a.shape, lambda i: (0, 0), pipeline_mode=pl.Buffered(1))
    w_router_cat = _split_weight(jnp.pad(w_router, ((0, 0), (0, LANES - ne))))
    args = (att, y, x, attn_gain.reshape(1, -1), w_att, w_y, post_gain.reshape(1, d),
            pre_ffn_gain.reshape(1, d), w_router_cat, b_router.reshape(1, ne))
    return pl.pallas_call(
        _out_proj_kernel,
        grid=(t // tm,),
        in_specs=[rows(att.shape[1]), rows(y.shape[1]), rows(d)] + [whole(a) for a in args[3:]],
        out_specs=[rows(d), pl.BlockSpec((tm * (d // PACKED_SLAB), LANES), lambda i: (i, 0)),
                   rows(TOP_K), rows(TOP_K)],
        out_shape=[jax.ShapeDtypeStruct((t, d), F32),
                   jax.ShapeDtypeStruct((t * (d // PACKED_SLAB), LANES), jnp.uint32),
                   jax.ShapeDtypeStruct((t, TOP_K), jnp.int32),
                   jax.ShapeDtypeStruct((t, TOP_K), F32)],
        compiler_params=_params("parallel"),
    )(*args)


DMA_ISSUE_UNROLL = 8


def _issue_row_copies(count, row_index, src_ref, dst_ref, sem, n):
    trips = lax.shift_right_logical(count + (DMA_ISSUE_UNROLL - 1), DMA_ISSUE_UNROLL.bit_length() - 1)

    def body(g, carry):
        for u in range(DMA_ISSUE_UNROLL):
            r = g * DMA_ISSUE_UNROLL + u
            pltpu.make_async_copy(src_ref.at[pl.ds(row_index(r) * n, n)],
                                  dst_ref.at[pl.ds(r * n, n)], sem).start(priority=u % 2)
        return carry

    lax.fori_loop(0, trips, body, 0)
    return trips


def _gather_kernel(tok_ref, first_ref, valid_ref, used_ref, src_ref, o_ref, buf_ref, sem):
    tm = o_ref.shape[0]
    n = buf_ref.shape[1] // tm
    i = pl.program_id(0)
    used = used_ref[0]
    last = tok_ref.shape[0] - 1

    group = DMA_ISSUE_UNROLL * n

    def issue(block, slot):
        first = first_ref[block]
        valid = valid_ref[block]
        token = lambda r: jnp.where(r < valid, tok_ref[jnp.clip(first + r, 0, last)], 0)
        _issue_row_copies(valid, token, src_ref, buf_ref.at[slot], sem.at[slot], n)

    def wait_group(g, slot):
        pltpu.make_async_copy(src_ref.at[pl.ds(0, group)], buf_ref.at[slot, pl.ds(g * group, group)],
                              sem.at[slot]).wait()

    @pl.when(i == 0)
    def _():
        buf_ref[...] = jnp.zeros(buf_ref.shape, buf_ref.dtype)

    for slot in range(2):
        @pl.when(jnp.logical_and(i % 2 == slot, i < used))
        def _(slot=slot):
            if slot == 0:
                @pl.when(i == 0)
                def _():
                    issue(0, 0)

            @pl.when(i + 1 < used)
            def _():
                issue(i + 1, 1 - slot)

            trips = lax.shift_right_logical(valid_ref[i] + (DMA_ISSUE_UNROLL - 1),
                                            DMA_ISSUE_UNROLL.bit_length() - 1)
            lax.fori_loop(0, trips, lambda g, c: (wait_group(g, slot), c)[1], 0)
            first_half, second_half = _unpack_bf16_pairs(_load_slabs(buf_ref.at[slot], 0, tm, n))
            half = o_ref.shape[1] // 2
            o_ref[:, :half] = first_half.astype(o_ref.dtype)
            o_ref[:, half:] = second_half.astype(o_ref.dtype)

    @pl.when(i >= used)
    def _():
        o_ref[...] = jnp.zeros(o_ref.shape, o_ref.dtype)


def _gather_rows(src_slabs, tok_sorted, block_first, block_valid, n_used, tm, d, out_dtype):
    n = d // PACKED_SLAB
    n_blocks = block_first.shape[0]
    return pl.pallas_call(
        _gather_kernel,
        grid_spec=pltpu.PrefetchScalarGridSpec(
            num_scalar_prefetch=4,
            grid=(n_blocks,),
            in_specs=[pl.BlockSpec(memory_space=pl.ANY)],
            out_specs=pl.BlockSpec((tm, d), lambda i, *_: (i, 0)),
            scratch_shapes=[pltpu.VMEM((2, tm * n, LANES), src_slabs.dtype),
                            pltpu.SemaphoreType.DMA((2,))],
        ),
        out_shape=jax.ShapeDtypeStruct((n_blocks * tm, d), out_dtype),
        compiler_params=_params("arbitrary"),
    )(tok_sorted, block_first, block_valid, n_used, src_slabs)


def _expert_changed(be_ref, i):
    return jnp.logical_or(i == 0, be_ref[i] != be_ref[jnp.maximum(i - 1, 0)])


def _weight_run_pipeline(be_ref, run_ref, run_expert_ref, nruns_ref, weights, slots, bf_copies, sem, tn):
    j = pl.program_id(0)
    i = pl.program_id(1)
    n_runs = nruns_ref[0]

    def copies(expert, col, slot):
        return [pltpu.make_async_copy(w.at[expert, :, pl.ds(pl.multiple_of(col * tn, tn), tn)],
                                      s.at[slot], sem.at[slot]) for w, s in zip(weights, slots)]

    @pl.when(_expert_changed(be_ref, i))
    def _():
        run = run_ref[i]
        seq = j * n_runs + run
        slot = lax.rem(seq, 2)

        @pl.when(seq == 0)
        def _():
            for c in copies(run_expert_ref[0], 0, 0):
                c.start()

        for c in copies(be_ref[i], j, slot):
            c.wait()
        more_runs = run + 1 < n_runs

        @pl.when(more_runs)
        def _():
            for c in copies(run_expert_ref[run + 1], j, 1 - slot):
                c.start()

        @pl.when(jnp.logical_and(jnp.logical_not(more_runs), j + 1 < pl.num_programs(0)))
        def _():
            for c in copies(run_expert_ref[0], j + 1, 1 - slot):
                c.start()

        for s, b in zip(slots, bf_copies):
            b[...] = s[slot].astype(BF16)


def _for_valid_rows(valid, tm, compute, zero_fill):
    half = tm // 2

    @pl.when(valid > half)
    def _():
        compute(tm)

    @pl.when(jnp.logical_and(valid > 0, valid <= half))
    def _():
        compute(half)
        zero_fill(half, tm)

    @pl.when(valid == 0)
    def _():
        zero_fill(0, tm)


def _moe_up_kernel(be_ref, valid_ref, run_ref, run_expert_ref, nruns_ref, x_ref, wg_hbm, wu_hbm,
                   bg_ref, bu_ref, h_ref, wg_slots, wu_slots, wg_bf, wu_bf, sem):
    i = pl.program_id(1)
    _weight_run_pipeline(be_ref, run_ref, run_expert_ref, nruns_ref, (wg_hbm, wu_hbm),
                         (wg_slots, wu_slots), (wg_bf, wu_bf), sem, h_ref.shape[1])

    def compute(rows):
        x = x_ref[0:rows, :]
        gate = jnp.minimum(_dot(x, wg_bf[...]) + bg_ref[0], SWIGLU_LIMIT)
        up = jnp.clip(_dot(x, wu_bf[...]) + bu_ref[0], -SWIGLU_LIMIT, SWIGLU_LIMIT)
        h_ref[0:rows, :] = ((up + 1.0) * (gate * _sigmoid(gate * SWIGLU_ALPHA))).astype(h_ref.dtype)

    def zero_fill(lo, hi):
        h_ref[lo:hi, :] = jnp.zeros((hi - lo, h_ref.shape[1]), h_ref.dtype)

    _for_valid_rows(valid_ref[i], x_ref.shape[0], compute, zero_fill)


def _moe_down_kernel(be_ref, valid_ref, run_ref, run_expert_ref, nruns_ref, h_ref, wd_hbm, bd_ref,
                     y_ref, wd_slots, wd_bf, sem):
    i = pl.program_id(1)
    tm = h_ref.shape[0]
    d = wd_bf.shape[1]
    half = d // 2
    n = d // PACKED_SLAB
    _weight_run_pipeline(be_ref, run_ref, run_expert_ref, nruns_ref, (wd_hbm,), (wd_slots,),
                         (wd_bf,), sem, d)

    def compute(rows):
        h = h_ref[0:rows, :]
        first = _dot(h, wd_bf[:, :half]) + bd_ref[0, :, :half]
        second = _dot(h, wd_bf[:, half:]) + bd_ref[0, :, half:]
        _store_slabs(y_ref, 0, _pack_bf16_pairs(first, second))

    def zero_fill(lo, hi):
        y_ref[lo * n:hi * n, :] = jnp.zeros(((hi - lo) * n, LANES), y_ref.dtype)

    _for_valid_rows(valid_ref[i], tm, compute, zero_fill)


def _moe_up(xs, tables, w_gate, b_gate, w_up, b_up, tm, tn):
    n_pad, d = xs.shape
    ne, _, ff = w_gate.shape
    hbm = pl.BlockSpec(memory_space=pl.ANY)
    b_spec = pl.BlockSpec((1, 1, tn), lambda j, i, be, *_: (be[i], 0, j))
    return pl.pallas_call(
        _moe_up_kernel,
        grid_spec=pltpu.PrefetchScalarGridSpec(
            num_scalar_prefetch=len(tables),
            grid=(ff // tn, n_pad // tm),
            in_specs=[pl.BlockSpec((tm, d), lambda j, i, *_: (i, 0)), hbm, hbm, b_spec, b_spec],
            out_specs=pl.BlockSpec((tm, tn), lambda j, i, *_: (i, j)),
            scratch_shapes=[pltpu.VMEM((2, d, tn), F32), pltpu.VMEM((2, d, tn), F32),
                            pltpu.VMEM((d, tn), BF16), pltpu.VMEM((d, tn), BF16),
                            pltpu.SemaphoreType.DMA((2,))],
        ),
        out_shape=jax.ShapeDtypeStruct((n_pad, ff), BF16),
        compiler_params=_params("arbitrary", "arbitrary"),
    )(*tables, xs, w_gate, w_up, b_gate.reshape(ne, 1, ff), b_up.reshape(ne, 1, ff))


def _moe_down(hs, tables, w_down, b_down, tm):
    n_pad, ff = hs.shape
    ne, _, d = w_down.shape
    n = d // PACKED_SLAB
    return pl.pallas_call(
        _moe_down_kernel,
        grid_spec=pltpu.PrefetchScalarGridSpec(
            num_scalar_prefetch=len(tables),
            grid=(1, n_pad // tm),
            in_specs=[pl.BlockSpec((tm, ff), lambda j, i, *_: (i, 0)),
                      pl.BlockSpec(memory_space=pl.ANY),
                      pl.BlockSpec((1, 1, d), lambda j, i, be, *_: (be[i], 0, 0))],
            out_specs=pl.BlockSpec((tm * n, LANES), lambda j, i, *_: (i, 0)),
            scratch_shapes=[pltpu.VMEM((2, ff, d), F32), pltpu.VMEM((ff, d), BF16),
                            pltpu.SemaphoreType.DMA((2,))],
        ),
        out_shape=jax.ShapeDtypeStruct((n_pad * n, LANES), jnp.uint32),
        compiler_params=_params("arbitrary", "arbitrary"),
    )(*tables, hs, w_down, b_down.reshape(ne, 1, d))


def _combine_ple_kernel(pos_ref, pos_next_ref, y_ref, w_ref, h_ref, p_ref, gffn_ref, gple_ref,
                        wgate_ref, wproj_ref, gpost_ref, o_ref, buf_ref, ff_ref, sem):
    tm, d = h_ref.shape
    n = d // PACKED_SLAB
    half = d // 2
    i = pl.program_id(0)

    def issue(idx_ref, slot):
        _issue_row_copies(TOP_K * tm, lambda r: idx_ref[0, 0, r], y_ref, buf_ref.at[slot],
                          sem.at[slot], n)

    for slot in range(2):
        @pl.when(i % 2 == slot)
        def _(slot=slot):
            if slot == 0:
                @pl.when(i == 0)
                def _():
                    issue(pos_ref, 0)

            @pl.when(i + 1 < pl.num_programs(0))
            def _():
                issue(pos_next_ref, 1 - slot)

            pltpu.make_async_copy(y_ref.at[pl.ds(0, TOP_K * tm * n)], buf_ref.at[slot],
                                  sem.at[slot]).wait()
            w = w_ref[...]
            first = second = None
            for kk in range(TOP_K):
                a, b = _unpack_bf16_pairs(_load_slabs(buf_ref.at[slot], kk * tm * n, tm, n))
                wk = w[:, kk:kk + 1]
                first = wk * a if first is None else first + wk * a
                second = wk * b if second is None else second + wk * b
            ff_ref[:, :half] = first
            ff_ref[:, half:] = second

    h = h_ref[...] + _rms(ff_ref[...], gffn_ref[...])
    gate = _sigmoid(_dot(_rms(h, gple_ref[...]).astype(BF16), wgate_ref[...]))
    emb = _dot(p_ref[...].astype(BF16), wproj_ref[...])
    o_ref[...] = h + _rms(gate * emb, gpost_ref[...])


def _combine_ple(y_rows, pos, top_w, h, p, ffn_gain, ple_gain, w_ple_gate, w_ple_proj,
                 ple_post_gain, tm):
    t, d = h.shape
    rows = lambda width: pl.BlockSpec((tm, width), lambda i: (i, 0))
    whole = lambda a: pl.BlockSpec(a.shape, lambda i: (0, 0), pipeline_mode=pl.Buffered(1))
    consts = (ffn_gain.reshape(1, d), ple_gain.reshape(1, d), w_ple_gate, w_ple_proj,
              ple_post_gain.reshape(1, d))
    steps = t // tm
    return pl.pallas_call(
        _combine_ple_kernel,
        grid=(steps,),
        in_specs=[
            pl.BlockSpec((1, 1, TOP_K * tm), lambda i: (i, 0, 0), memory_space=pltpu.SMEM),
            pl.BlockSpec((1, 1, TOP_K * tm), lambda i: (jnp.minimum(i + 1, steps - 1), 0, 0),
                         memory_space=pltpu.SMEM),
            pl.BlockSpec(memory_space=pl.ANY),
            rows(TOP_K), rows(d), rows(p.shape[1]),
        ] + [whole(a) for a in consts],
        out_specs=rows(d),
        out_shape=jax.ShapeDtypeStruct((t, d), F32),
        scratch_shapes=[pltpu.VMEM((2, TOP_K * tm * (d // PACKED_SLAB), LANES), jnp.uint32),
                        pltpu.VMEM((tm, d), F32), pltpu.SemaphoreType.DMA((2,))],
        compiler_params=_params("arbitrary"),
    )(pos, pos, y_rows, top_w, h, p, *consts)


def _routing_tables(top_e, tm, tok_block):
    t = top_e.shape[0]
    n = t * TOP_K
    n_blocks = n // tm + N_EXPERTS
    e_flat = top_e.reshape(n)
    experts = jnp.arange(N_EXPERTS, dtype=jnp.int32)
    order = jnp.argsort(e_flat, stable=True).astype(jnp.int32)
    rank = jnp.argsort(order).astype(jnp.int32)
    onehot = e_flat[:, None] == experts[None, :]
    counts = jnp.sum(onehot, axis=0, dtype=jnp.int32)
    group_start = jnp.cumsum(counts) - counts
    padded = ((counts + tm - 1) // tm) * tm
    padded_end = jnp.cumsum(padded)
    padded_start = padded_end - padded
    shift = padded_start - group_start
    pos = (rank + jnp.sum(jnp.where(onehot, shift[None, :], 0), axis=1)).reshape(t, TOP_K)
    n_used = (padded_end[-1] // tm).astype(jnp.int32).reshape(1)
    offs = jnp.arange(n_blocks, dtype=jnp.int32) * tm
    block_e = jnp.sum(padded_end[None, :] <= offs[:, None], axis=1, dtype=jnp.int32)
    last_e = jnp.sum(padded_end <= (n_used[0] - 1) * tm, dtype=jnp.int32)
    block_e = jnp.minimum(jnp.where(offs < padded_end[-1], block_e, last_e), N_EXPERTS - 1)
    sel = block_e[:, None] == experts[None, :]
    pick = lambda table: jnp.sum(jnp.where(sel, table[None, :], 0), axis=1)
    in_group = offs - pick(padded_start)
    block_first = (pick(group_start) + in_group).astype(jnp.int32)
    block_valid = jnp.where(offs < padded_end[-1], jnp.clip(pick(counts) - in_group, 0, tm), 0)
    present = counts > 0
    run_of_expert = jnp.cumsum(present, dtype=jnp.int32) - 1
    n_runs = jnp.sum(present, dtype=jnp.int32).reshape(1)
    block_run = pick(run_of_expert).astype(jnp.int32)
    runs = jnp.arange(N_EXPERTS + 1, dtype=jnp.int32)
    run_expert = jnp.sum(jnp.where(present[None, :] & (run_of_expert[None, :] == runs[:, None]),
                                   experts[None, :], 0), axis=1, dtype=jnp.int32)
    pos_blocks = pos.reshape(t // tok_block, tok_block, TOP_K).transpose(0, 2, 1)
    moe_tables = (block_e, block_valid.astype(jnp.int32), block_run, run_expert, n_runs)
    return (order // TOP_K, block_first, pos_blocks.reshape(t // tok_block, 1, TOP_K * tok_block),
            n_used, moe_tables)


def _layer(h, p, pre_mix_norm, w_in, b_fgate, conv_w, conv_b, dt_bias, a_log, d_skip,
           attn_out_norm, ssm_out_norm, w_out, post_mix_norm, pre_ffn_norm, w_router, b_router,
           w_gate, b_gate, w_up, b_up, w_down, b_down, post_ffn_norm, ple_norm, w_ple_gate,
           w_ple_proj, ple_post_norm):
    bsz, seq, d = h.shape
    t = bsz * seq
    att_w = ATT_HEADS * ATT_HEAD_DIM
    heads = dt_bias.shape[0]
    ssm_w = heads * SSM_HEAD_DIM
    conv_ch = conv_w.shape[1]
    o_fg = 3 * att_w
    o_z = o_fg + ATT_HEADS
    o_xbc = o_z + ssm_w
    o_dt = o_xbc + conv_ch
    x2 = h.reshape(t, d)

    q_scale = LOG2E * ATT_HEAD_DIM ** -0.5
    w_qkv = jnp.concatenate([w_in[:, :att_w] * q_scale, w_in[:, att_w:o_fg]], axis=1).astype(BF16)
    w_z = w_in[:, o_z:o_xbc].astype(BF16)
    w_xbc = w_in[:, o_xbc:o_dt].astype(BF16)
    small_pad = LANES - ATT_HEADS - heads
    w_small = jnp.concatenate([w_in[:, o_fg:o_z], w_in[:, o_dt:], jnp.zeros((d, small_pad), F32)], axis=1)

    tm = min(1024, t)
    qkv = _rms_matmul(x2, pre_mix_norm, w_qkv, BF16, tm, 1024)
    zz = _rms_matmul(x2, pre_mix_norm, w_z, F32, tm, 1024)
    xbc = _rms_matmul(x2, pre_mix_norm, w_xbc, F32, tm, 1024)
    small = _rms_matmul(x2, pre_mix_norm, _split_weight(w_small), F32, tm, LANES, split=True)
    small = small.reshape(bsz, seq, LANES)

    fg_t = jnp.swapaxes(small[:, :, :ATT_HEADS], 1, 2)
    dt_raw = small[:, :, ATT_HEADS:ATT_HEADS + heads]
    c_t = _fg_cumsum(fg_t, b_fgate)
    att = _fox_attention(qkv.reshape(bsz, seq, 3 * att_w), c_t, min(512, seq))
    y = _ssd(xbc.reshape(bsz, seq, conv_ch), zz.reshape(bsz, seq, ssm_w), dt_raw,
             jnp.swapaxes(dt_raw, 1, 2), conv_w, conv_b, dt_bias, a_log, d_skip, ssm_out_norm)

    w_out_bf = w_out.astype(BF16)
    h1, hn, top_e, top_w = _out_proj(att.reshape(t, att_w), y.reshape(t, ssm_w), x2, attn_out_norm,
                                     w_out_bf[:att_w], w_out_bf[att_w:], post_mix_norm,
                                     pre_ffn_norm, w_router, b_router, min(512, t))

    tok_block = min(256, t)
    tok_sorted, block_first, pos, n_used, moe_tables = _routing_tables(top_e, MOE_ROWS, tok_block)
    xs = _gather_rows(hn, tok_sorted, block_first, moe_tables[1], n_used, MOE_ROWS, d, BF16)
    hs = _moe_up(xs, moe_tables, w_gate, b_gate, w_up, b_up, MOE_ROWS, min(1024, w_gate.shape[2]))
    y_rows = _moe_down(hs, moe_tables, w_down, b_down, MOE_ROWS)
    out = _combine_ple(y_rows, pos, top_w, h1, p.reshape(t, -1), post_ffn_norm, ple_norm,
                       w_ple_gate.astype(BF16), w_ple_proj.astype(BF16), ple_post_norm, tok_block)
    return out.reshape(bsz, seq, d)


def kernel(x, p, pre_mix_norm, w_in, b_fgate, conv_w, conv_b, dt_bias, a_log, d_skip, attn_out_norm, ssm_out_norm, w_out, post_mix_norm, pre_ffn_norm, w_router, b_router, w_gate, b_gate, w_up, b_up, w_down, b_down, post_ffn_norm, ple_norm, w_ple_gate, w_ple_proj, ple_post_norm):
    h = x
    for i in range(p.shape[0]):
        h = _layer(h, p[i], pre_mix_norm[i], w_in[i], b_fgate[i], conv_w[i], conv_b[i], dt_bias[i],
                   a_log[i], d_skip[i], attn_out_norm[i], ssm_out_norm[i], w_out[i],
                   post_mix_norm[i], pre_ffn_norm[i], w_router[i], b_router[i], w_gate[i], b_gate[i],
                   w_up[i], b_up[i], w_down[i], b_down[i], post_ffn_norm[i], ple_norm[i],
                   w_ple_gate[i], w_ple_proj[i], ple_post_norm[i])
    return h
```

```python
import functools

import jax
import jax.numpy as jnp
from jax import lax
from jax.experimental import pallas as pl
from jax.experimental.pallas import tpu as pltpu

F32 = jnp.float32
BF16 = jnp.bfloat16

ATT_HEADS = 8
ATT_HEAD_DIM = 128
SSM_HEAD_DIM = 64
SSM_GROUPS = 8
SSM_STATE = 128
SSM_CHUNK = 128
CONV_WIDTH = 4
N_EXPERTS = 32
TOP_K = 4
SWIGLU_LIMIT = 7.0
SWIGLU_ALPHA = 1.702
EPS = 1e-6
LOG2E = 1.4426950408889634

LANES = 128
SUBLANES = 8
VMEM_LIMIT_BYTES = 56 * 1024 * 1024

MOE_ROWS = 512


def _params(*semantics):
    return pltpu.CompilerParams(dimension_semantics=semantics, vmem_limit_bytes=VMEM_LIMIT_BYTES)


def _rms(x, gain):
    return x * lax.rsqrt(jnp.mean(x * x, axis=-1, keepdims=True) + EPS) * gain


def _sigmoid(x):
    return 1.0 / (1.0 + jnp.exp(-x))


def _softplus(x):
    return jnp.maximum(x, 0.0) + jnp.log1p(jnp.exp(-jnp.abs(x)))


def _split3(x):
    x1 = x.astype(BF16)
    r1 = x - x1.astype(F32)
    x2 = r1.astype(BF16)
    x3 = (r1 - x2.astype(F32)).astype(BF16)
    return x1, x2, x3


def _dot(a, b):
    return jnp.dot(a, b, preferred_element_type=F32)


def _dot_exact_rhs(x, m):
    x1, x2, x3 = _split3(x)
    return _dot(x1, m) + _dot(x2, m) + _dot(x3, m)


def _dot_exact_lhs(m, x):
    x1, x2, x3 = _split3(x)
    return _dot(m, x1) + _dot(m, x2) + _dot(m, x3)


def _split_weight(w):
    hi = w.astype(BF16)
    return jnp.concatenate([hi, (w - hi.astype(F32)).astype(BF16)], axis=1)


def _dot_split(x, w_cat):
    n = w_cat.shape[1] // 2
    x_hi = x.astype(BF16)
    x_lo = (x - x_hi.astype(F32)).astype(BF16)
    a = _dot(x_hi, w_cat)
    return a[:, :n] + a[:, n:] + _dot(x_lo, w_cat[:, :n])


def _store_slabs(ref, base, value):
    rows, width = value.shape
    n = width // LANES
    for c in range(n):
        ref[pl.ds(base + c, rows, stride=n), :] = value[:, c * LANES:(c + 1) * LANES]


PACKED_SLAB = 2 * LANES


def _pack_bf16_pairs(first, second):
    as_bits = lambda v: lax.bitcast_convert_type(v.astype(BF16).astype(F32), jnp.uint32)
    return as_bits(first) | (as_bits(second) >> 16)


def _unpack_bf16_pairs(u):
    return (lax.bitcast_convert_type(u & jnp.uint32(0xFFFF0000), F32),
            lax.bitcast_convert_type(u << 16, F32))


def _load_slabs(ref, base, rows, n):
    return jnp.concatenate([ref[pl.ds(base + c, rows, stride=n), :] for c in range(n)], axis=1)


def _rms_matmul_kernel(x_ref, g_ref, w_ref, o_ref, xn_ref):
    @pl.when(pl.program_id(1) == 0)
    def _():
        xn_ref[...] = _rms(x_ref[...], g_ref[...]).astype(xn_ref.dtype)

    if xn_ref.dtype == F32:
        acc = _dot_split(xn_ref[...], w_ref[...])
    else:
        acc = _dot(xn_ref[...], w_ref[...])
    o_ref[...] = acc.astype(o_ref.dtype)


def _rms_matmul(x, gain, w, out_dtype, tm, tn, split=False):
    m, k = x.shape
    n = w.shape[1] // 2 if split else w.shape[1]
    assert not split or n == tn
    return pl.pallas_call(
        _rms_matmul_kernel,
        grid=(m // tm, n // tn),
        in_specs=[
            pl.BlockSpec((tm, k), lambda i, j: (i, 0)),
            pl.BlockSpec((1, k), lambda i, j: (0, 0)),
            pl.BlockSpec((k, 2 * tn if split else tn), lambda i, j: (0, j)),
        ],
        out_specs=pl.BlockSpec((tm, tn), lambda i, j: (i, j)),
        out_shape=jax.ShapeDtypeStruct((m, n), out_dtype),
        scratch_shapes=[pltpu.VMEM((tm, k), F32 if split else BF16)],
        compiler_params=_params("parallel", "arbitrary"),
    )(x, gain.reshape(1, k), w)


def _fg_cumsum_kernel(fg_ref, b_ref, c_ref):
    seq = fg_ref.shape[2]
    row = lax.broadcasted_iota(jnp.int32, (LANES, LANES), 0)
    col = lax.broadcasted_iota(jnp.int32, (LANES, LANES), 1)
    triu = (row <= col).astype(BF16)
    carry = jnp.zeros((fg_ref.shape[1], 1), F32)
    for ci in range(seq // LANES):
        z = fg_ref[0, :, ci * LANES:(ci + 1) * LANES] + b_ref[...]
        ls = jnp.minimum(z, 0.0) - jnp.log1p(jnp.exp(-jnp.abs(z)))
        cs = _dot_exact_rhs(ls, triu) + carry
        c_ref[0, :, ci * LANES:(ci + 1) * LANES] = cs * LOG2E
        carry = cs[:, LANES - 1:LANES]


def _fg_cumsum(fg_t, b_fgate):
    bsz, heads, seq = fg_t.shape
    return pl.pallas_call(
        _fg_cumsum_kernel,
        grid=(bsz,),
        in_specs=[
            pl.BlockSpec((1, heads, seq), lambda b: (b, 0, 0)),
            pl.BlockSpec((heads, 1), lambda b: (0, 0)),
        ],
        out_specs=pl.BlockSpec((1, heads, seq), lambda b: (b, 0, 0)),
        out_shape=jax.ShapeDtypeStruct((bsz, heads, seq), F32),
        compiler_params=_params("parallel"),
    )(fg_t, b_fgate.reshape(heads, 1))


def _attn_kernel(q_ref, k_ref, v_ref, ck_ref, o_ref, *, blk):
    qi = pl.program_id(2)
    q = q_ref[0]

    def step(j, carry, masked):
        m, l, acc = carry
        start = pl.multiple_of(j * blk, blk)
        kj = k_ref[0, pl.ds(start, blk), :]
        vj = v_ref[0, pl.ds(start, blk), :]
        s = lax.dot_general(q, kj, (((1,), (1,)), ((), ())), preferred_element_type=F32)
        s = s - ck_ref[0, 0, j]
        if masked:
            row = lax.broadcasted_iota(jnp.int32, (blk, blk), 0)
            col = lax.broadcasted_iota(jnp.int32, (blk, blk), 1)
            s = jnp.where(col <= row, s, -jnp.inf)
        m_new = jnp.maximum(m, jnp.max(s, axis=-1, keepdims=True))
        alpha = jnp.exp2(m - m_new)
        p = jnp.exp2(s - m_new)
        l = alpha * l + jnp.sum(p, axis=-1, keepdims=True)
        acc = alpha * acc + _dot(p.astype(BF16), vj)
        return m_new, l, acc

    init = (jnp.full((blk, 1), -jnp.inf, F32), jnp.zeros((blk, 1), F32),
            jnp.zeros((blk, q.shape[-1]), F32))
    carry = lax.fori_loop(0, qi, lambda j, c: step(j, c, False), init)
    _, l, acc = step(qi, carry, True)
    o_ref[0] = (acc / l).astype(o_ref.dtype)


def _fox_attention(qkv, c_t, blk):
    bsz, seq, _ = qkv.shape
    h, dh = ATT_HEADS, ATT_HEAD_DIM
    ck = c_t.reshape(bsz, h, seq // blk, 1, blk)
    return pl.pallas_call(
        functools.partial(_attn_kernel, blk=blk),
        grid=(bsz, h, seq // blk),
        in_specs=[
            pl.BlockSpec((1, blk, dh), lambda b, hh, i: (b, i, hh)),
            pl.BlockSpec((1, seq, dh), lambda b, hh, i: (b, 0, h + hh)),
            pl.BlockSpec((1, seq, dh), lambda b, hh, i: (b, 0, 2 * h + hh)),
            pl.BlockSpec((1, 1, seq // blk, 1, blk), lambda b, hh, i: (b, hh, 0, 0, 0)),
        ],
        out_specs=pl.BlockSpec((1, blk, dh), lambda b, hh, i: (b, i, hh)),
        out_shape=jax.ShapeDtypeStruct((bsz, seq, h * dh), BF16),
        compiler_params=_params("parallel", "parallel", "arbitrary"),
    )(qkv, qkv, qkv, ck)


def _ssd_kernel(xbc_ref, z_ref, dt_ref, dtt_ref, cw_ref, cb_ref, dtb_ref, dtbt_ref, alog_ref,
                alogt_ref, dskip_ref, gain_ref, expand_ref, y_ref, ext_ref, act_ref, state_ref):
    chunk = SSM_CHUNK
    width = y_ref.shape[2]
    heads = dt_ref.shape[2]
    per_group = heads // SSM_GROUPS
    gw = per_group * SSM_HEAD_DIM
    nstate = SSM_STATE
    halo = SUBLANES
    c_idx = pl.program_id(1)

    @pl.when(c_idx == 0)
    def _():
        ext_ref[0:halo, :] = jnp.zeros((halo, ext_ref.shape[1]), F32)
        state_ref[...] = jnp.zeros(state_ref.shape, F32)

    ext_ref[halo:halo + chunk, :] = xbc_ref[0]
    conv_ch = ext_ref.shape[1]
    col_tile = 4 * LANES
    for ct in range(conv_ch // col_tile):
        sl = slice(ct * col_tile, (ct + 1) * col_tile)
        acc = cb_ref[:, sl] + cw_ref[CONV_WIDTH - 1:CONV_WIDTH, sl] * ext_ref[halo:halo + chunk, sl]
        for kk in range(CONV_WIDTH - 1):
            off = halo - (CONV_WIDTH - 1) + kk
            acc = acc + cw_ref[kk:kk + 1, sl] * ext_ref[off:off + chunk, sl]
        act_ref[:, sl] = acc * _sigmoid(acc)
    ext_ref[0:halo, :] = xbc_ref[0, chunk - halo:chunk, :]

    row = lax.broadcasted_iota(jnp.int32, (chunk, chunk), 0)
    col = lax.broadcasted_iota(jnp.int32, (chunk, chunk), 1)
    causal = row >= col
    tril = causal.astype(BF16)
    triu = (row <= col).astype(BF16)
    dt = _softplus(dt_ref[0] + dtb_ref[...])
    adt = dt * (-jnp.exp(alog_ref[...]))
    acs = _dot_exact_lhs(tril, adt)
    dtt = _softplus(dtt_ref[0] + dtbt_ref[...])
    acs_t = _dot_exact_rhs(dtt * (-jnp.exp(alogt_ref[...])), triu)

    expand = expand_ref[...]
    dt_x = _dot_exact_rhs(dt, expand)
    acs_x = _dot_exact_rhs(acs, expand)
    last_x = acs_x[chunk - 1:chunk, :]
    xs = act_ref[:, 0:width]
    xc = xs * dt_x
    xc_bf = xc.astype(BF16)
    xcd_bf = (xc * jnp.exp(last_x - acs_x)).astype(BF16)
    in_decay = jnp.exp(acs_x)
    chunk_decay = jnp.exp(last_x)
    lane = lax.broadcasted_iota(jnp.int32, (chunk, LANES), 1)
    low_half = lane < SSM_HEAD_DIM

    for g in range(SSM_GROUPS):
        gsl = slice(g * gw, (g + 1) * gw)
        b_g = act_ref[:, width + g * nstate:width + (g + 1) * nstate]
        c_g = act_ref[:, width + (SSM_GROUPS + g) * nstate:width + (SSM_GROUPS + g + 1) * nstate]
        c_bf = c_g.astype(BF16)
        cb = lax.dot_general(c_bf, b_g.astype(BF16), (((1,), (1,)), ((), ())),
                             preferred_element_type=F32)
        pieces = []
        for pair in range(per_group // 2):
            ms = []
            for r in (2 * pair, 2 * pair + 1):
                hd = g * per_group + r
                seg = acs[:, hd:hd + 1] - acs_t[hd:hd + 1, :]
                decay = jnp.exp(jnp.where(causal, seg, -jnp.inf))
                ms.append((cb * decay).astype(BF16))
            xpair = xc_bf[:, g * gw + pair * LANES:g * gw + (pair + 1) * LANES]
            zero = jnp.zeros_like(xpair)
            rhs = jnp.concatenate([jnp.where(low_half, xpair, zero),
                                   jnp.where(low_half, zero, xpair)], axis=0)
            pieces.append(_dot(jnp.concatenate(ms, axis=1), rhs))
        y_diag = jnp.concatenate(pieces, axis=1)
        state = state_ref[g]
        y_off = _dot(c_bf, state.astype(BF16)) * in_decay[:, gsl]
        y = y_diag + y_off + dskip_ref[:, gsl] * xs[:, gsl]
        zg = z_ref[0, :, gsl]
        y = y * (zg * _sigmoid(zg))
        y = y * lax.rsqrt(jnp.mean(y * y, axis=-1, keepdims=True) + EPS) * gain_ref[:, gsl]
        y_ref[0, :, gsl] = y.astype(y_ref.dtype)
        state_ref[g] = state * chunk_decay[:, gsl] + _dot(b_g.T.astype(BF16), xcd_bf[:, gsl])


def _ssd(xbc, z, dt_raw, dt_raw_t, conv_w, conv_b, dt_bias, a_log, d_skip, gain):
    bsz, seq, conv_ch = xbc.shape
    width = z.shape[2]
    heads = dt_raw.shape[2]
    chunk = SSM_CHUNK
    gw = width // SSM_GROUPS
    expand = (jnp.arange(width)[None, :] // SSM_HEAD_DIM == jnp.arange(heads)[:, None]).astype(BF16)
    dskip_x = jnp.repeat(d_skip, SSM_HEAD_DIM).reshape(1, width)
    const = lambda shape: pl.BlockSpec(shape, lambda b, c: (0,) * len(shape))
    return pl.pallas_call(
        _ssd_kernel,
        grid=(bsz, seq // chunk),
        in_specs=[
            pl.BlockSpec((1, chunk, conv_ch), lambda b, c: (b, c, 0)),
            pl.BlockSpec((1, chunk, width), lambda b, c: (b, c, 0)),
            pl.BlockSpec((1, chunk, heads), lambda b, c: (b, c, 0)),
            pl.BlockSpec((1, heads, chunk), lambda b, c: (b, 0, c)),
            const((CONV_WIDTH, conv_ch)),
            const((1, conv_ch)),
            const((1, heads)),
            const((heads, 1)),
            const((1, heads)),
            const((heads, 1)),
            const((1, width)),
            const((1, width)),
            const((heads, width)),
        ],
        out_specs=pl.BlockSpec((1, chunk, width), lambda b, c: (b, c, 0)),
        out_shape=jax.ShapeDtypeStruct((bsz, seq, width), BF16),
        scratch_shapes=[
            pltpu.VMEM((SUBLANES + chunk, conv_ch), F32),
            pltpu.VMEM((chunk, conv_ch), F32),
            pltpu.VMEM((SSM_GROUPS, SSM_STATE, gw), F32),
        ],
        compiler_params=_params("parallel", "arbitrary"),
    )(xbc, z, dt_raw, dt_raw_t, conv_w, conv_b.reshape(1, conv_ch), dt_bias.reshape(1, heads),
      dt_bias.reshape(heads, 1), a_log.reshape(1, heads), a_log.reshape(heads, 1), dskip_x,
      gain.reshape(1, width), expand)


def _top_k_softmax(l):
    rows, ne = l.shape
    lane = lax.broadcasted_iota(jnp.int32, l.shape, 1)
    slot = lax.broadcasted_iota(jnp.int32, (rows, TOP_K), 1)
    e_out = jnp.zeros((rows, TOP_K), jnp.int32)
    p_out = jnp.zeros((rows, TOP_K), F32)
    top = None
    for kk in range(TOP_K):
        m = jnp.max(l, axis=-1, keepdims=True)
        idx = jnp.min(jnp.where(l == m, lane, ne), axis=-1, keepdims=True)
        if top is None:
            top = m
        e_out = jnp.where(slot == kk, idx, e_out)
        p_out = jnp.where(slot == kk, jnp.exp(m - top), p_out)
        l = jnp.where(lane == idx, -jnp.inf, l)
    return e_out, p_out / jnp.sum(p_out, axis=-1, keepdims=True)


def _out_proj_kernel(att_ref, y_ref, x_ref, ga_ref, wa_ref, wy_ref, gpost_ref, gpre_ref, wr_ref,
                     br_ref, h_ref, hn_ref, e_ref, w_ref):
    att_n = _rms(att_ref[...].astype(F32), ga_ref[...]).astype(BF16)
    mixed = _dot(att_n, wa_ref[...]) + _dot(y_ref[...], wy_ref[...])
    h = x_ref[...] + _rms(mixed, gpost_ref[...])
    h_ref[...] = h
    hn = _rms(h, gpre_ref[...])
    half = hn.shape[1] // 2
    _store_slabs(hn_ref, 0, _pack_bf16_pairs(hn[:, :half], hn[:, half:]))
    ne = br_ref.shape[1]
    e_ref[...], w_ref[...] = _top_k_softmax(_dot_split(hn, wr_ref[...])[:, :ne] + br_ref[...])


def _out_proj(att, y, x, attn_gain, w_att, w_y, post_gain, pre_ffn_gain, w_router, b_router, tm):
    t, d = x.shape
    ne = w_router.shape[1]
    rows = lambda width: pl.BlockSpec((tm, width), lambda i: (i, 0))
    whole = lambda a: pl.BlockSpec(a.shape, lambda i: (0, 0), pipeline_mode=pl.Buffered(1))
    w_router_cat = _split_weight(jnp.pad(w_router, ((0, 0), (0, LANES - ne))))
    args = (att, y, x, attn_gain.reshape(1, -1), w_att, w_y, post_gain.reshape(1, d),
            pre_ffn_gain.reshape(1, d), w_router_cat, b_router.reshape(1, ne))
    return pl.pallas_call(
        _out_proj_kernel,
        grid=(t // tm,),
        in_specs=[rows(att.shape[1]), rows(y.shape[1]), rows(d)] + [whole(a) for a in args[3:]],
        out_specs=[rows(d), pl.BlockSpec((tm * (d // PACKED_SLAB), LANES), lambda i: (i, 0)),
                   rows(TOP_K), rows(TOP_K)],
        out_shape=[jax.ShapeDtypeStruct((t, d), F32),
                   jax.ShapeDtypeStruct((t * (d // PACKED_SLAB), LANES), jnp.uint32),
                   jax.ShapeDtypeStruct((t, TOP_K), jnp.int32),
                   jax.ShapeDtypeStruct((t, TOP_K), F32)],
        compiler_params=_params("parallel"),
    )(*args)


DMA_ISSUE_UNROLL = 8


def _issue_row_copies(count, row_index, src_ref, dst_ref, sem, n):
    trips = lax.shift_right_logical(count + (DMA_ISSUE_UNROLL - 1), DMA_ISSUE_UNROLL.bit_length() - 1)

    def body(g, carry):
        for u in range(DMA_ISSUE_UNROLL):
            r = g * DMA_ISSUE_UNROLL + u
            pltpu.make_async_copy(src_ref.at[pl.ds(row_index(r) * n, n)],
                                  dst_ref.at[pl.ds(r * n, n)], sem).start(priority=u % 2)
        return carry

    lax.fori_loop(0, trips, body, 0)
    return trips


def _gather_kernel(tok_ref, first_ref, valid_ref, used_ref, src_ref, o_ref, buf_ref, sem):
    tm = o_ref.shape[0]
    n = buf_ref.shape[1] // tm
    i = pl.program_id(0)
    used = used_ref[0]
    last = tok_ref.shape[0] - 1

    group = DMA_ISSUE_UNROLL * n

    def issue(block, slot):
        first = first_ref[block]
        valid = valid_ref[block]
        token = lambda r: jnp.where(r < valid, tok_ref[jnp.clip(first + r, 0, last)], 0)
        _issue_row_copies(valid, token, src_ref, buf_ref.at[slot], sem.at[slot], n)

    def wait_group(g, slot):
        pltpu.make_async_copy(src_ref.at[pl.ds(0, group)], buf_ref.at[slot, pl.ds(g * group, group)],
                              sem.at[slot]).wait()

    @pl.when(i == 0)
    def _():
        buf_ref[...] = jnp.zeros(buf_ref.shape, buf_ref.dtype)

    for slot in range(2):
        @pl.when(jnp.logical_and(i % 2 == slot, i < used))
        def _(slot=slot):
            if slot == 0:
                @pl.when(i == 0)
                def _():
                    issue(0, 0)

            @pl.when(i + 1 < used)
            def _():
                issue(i + 1, 1 - slot)

            trips = lax.shift_right_logical(valid_ref[i] + (DMA_ISSUE_UNROLL - 1),
                                            DMA_ISSUE_UNROLL.bit_length() - 1)
            lax.fori_loop(0, trips, lambda g, c: (wait_group(g, slot), c)[1], 0)
            first_half, second_half = _unpack_bf16_pairs(_load_slabs(buf_ref.at[slot], 0, tm, n))
            half = o_ref.shape[1] // 2
            o_ref[:, :half] = first_half.astype(o_ref.dtype)
            o_ref[:, half:] = second_half.astype(o_ref.dtype)

    @pl.when(i >= used)
    def _():
        o_ref[...] = jnp.zeros(o_ref.shape, o_ref.dtype)


def _gather_rows(src_slabs, tok_sorted, block_first, block_valid, n_used, tm, d, out_dtype):
    n = d // PACKED_SLAB
    n_blocks = block_first.shape[0]
    return pl.pallas_call(
        _gather_kernel,
        grid_spec=pltpu.PrefetchScalarGridSpec(
            num_scalar_prefetch=4,
            grid=(n_blocks,),
            in_specs=[pl.BlockSpec(memory_space=pl.ANY)],
            out_specs=pl.BlockSpec((tm, d), lambda i, *_: (i, 0)),
            scratch_shapes=[pltpu.VMEM((2, tm * n, LANES), src_slabs.dtype),
                            pltpu.SemaphoreType.DMA((2,))],
        ),
        out_shape=jax.ShapeDtypeStruct((n_blocks * tm, d), out_dtype),
        compiler_params=_params("arbitrary"),
    )(tok_sorted, block_first, block_valid, n_used, src_slabs)


def _expert_changed(be_ref, i):
    return jnp.logical_or(i == 0, be_ref[i] != be_ref[jnp.maximum(i - 1, 0)])


def _weight_run_pipeline(be_ref, run_ref, run_expert_ref, nruns_ref, weights, slots, bf_copies, sem, tn):
    j = pl.program_id(0)
    i = pl.program_id(1)
    n_runs = nruns_ref[0]

    def copies(expert, col, slot):
        return [pltpu.make_async_copy(w.at[expert, :, pl.ds(pl.multiple_of(col * tn, tn), tn)],
                                      s.at[slot], sem.at[slot]) for w, s in zip(weights, slots)]

    @pl.when(_expert_changed(be_ref, i))
    def _():
        run = run_ref[i]
        seq = j * n_runs + run
        slot = lax.rem(seq, 2)

        @pl.when(seq == 0)
        def _():
            for c in copies(run_expert_ref[0], 0, 0):
                c.start()

        for c in copies(be_ref[i], j, slot):
            c.wait()
        more_runs = run + 1 < n_runs

        @pl.when(more_runs)
        def _():
            for c in copies(run_expert_ref[run + 1], j, 1 - slot):
                c.start()

        @pl.when(jnp.logical_and(jnp.logical_not(more_runs), j + 1 < pl.num_programs(0)))
        def _():
            for c in copies(run_expert_ref[0], j + 1, 1 - slot):
                c.start()

        for s, b in zip(slots, bf_copies):
            b[...] = s[slot].astype(BF16)


def _for_valid_rows(valid, tm, compute, zero_fill):
    quarter = tm // 4
    for part in range(1, 5):
        @pl.when(jnp.logical_and(valid > (part - 1) * quarter, valid <= part * quarter))
        def _(rows=part * quarter):
            compute(rows)
            if rows < tm:
                zero_fill(rows, tm)

    @pl.when(valid == 0)
    def _():
        zero_fill(0, tm)


def _moe_up_kernel(be_ref, valid_ref, run_ref, run_expert_ref, nruns_ref, x_ref, wg_hbm, wu_hbm,
                   bg_ref, bu_ref, h_ref, wg_slots, wu_slots, wg_bf, wu_bf, sem):
    i = pl.program_id(1)
    _weight_run_pipeline(be_ref, run_ref, run_expert_ref, nruns_ref, (wg_hbm, wu_hbm),
                         (wg_slots, wu_slots), (wg_bf, wu_bf), sem, h_ref.shape[1])

    def compute(rows):
        x = x_ref[0:rows, :]
        gate = jnp.minimum(_dot(x, wg_bf[...]) + bg_ref[0], SWIGLU_LIMIT)
        up = jnp.clip(_dot(x, wu_bf[...]) + bu_ref[0], -SWIGLU_LIMIT, SWIGLU_LIMIT)
        h_ref[0:rows, :] = ((up + 1.0) * (gate * _sigmoid(gate * SWIGLU_ALPHA))).astype(h_ref.dtype)

    def zero_fill(lo, hi):
        h_ref[lo:hi, :] = jnp.zeros((hi - lo, h_ref.shape[1]), h_ref.dtype)

    _for_valid_rows(valid_ref[i], x_ref.shape[0], compute, zero_fill)


def _moe_down_kernel(be_ref, valid_ref, run_ref, run_expert_ref, nruns_ref, h_ref, wd_hbm, bd_ref,
                     y_ref, wd_slots, wd_bf, sem):
    i = pl.program_id(1)
    tm = h_ref.shape[0]
    d = wd_bf.shape[1]
    half = d // 2
    n = d // PACKED_SLAB
    _weight_run_pipeline(be_ref, run_ref, run_expert_ref, nruns_ref, (wd_hbm,), (wd_slots,),
                         (wd_bf,), sem, d)

    def compute(rows):
        h = h_ref[0:rows, :]
        first = _dot(h, wd_bf[:, :half]) + bd_ref[0, :, :half]
        second = _dot(h, wd_bf[:, half:]) + bd_ref[0, :, half:]
        _store_slabs(y_ref, 0, _pack_bf16_pairs(first, second))

    def zero_fill(lo, hi):
        y_ref[lo * n:hi * n, :] = jnp.zeros(((hi - lo) * n, LANES), y_ref.dtype)

    _for_valid_rows(valid_ref[i], tm, compute, zero_fill)


def _moe_up(xs, tables, w_gate, b_gate, w_up, b_up, tm, tn):
    n_pad, d = xs.shape
    ne, _, ff = w_gate.shape
    hbm = pl.BlockSpec(memory_space=pl.ANY)
    b_spec = pl.BlockSpec((1, 1, tn), lambda j, i, be, *_: (be[i], 0, j))
    return pl.pallas_call(
        _moe_up_kernel,
        grid_spec=pltpu.PrefetchScalarGridSpec(
            num_scalar_prefetch=len(tables),
            grid=(ff // tn, n_pad // tm),
            in_specs=[pl.BlockSpec((tm, d), lambda j, i, *_: (i, 0)), hbm, hbm, b_spec, b_spec],
            out_specs=pl.BlockSpec((tm, tn), lambda j, i, *_: (i, j)),
            scratch_shapes=[pltpu.VMEM((2, d, tn), F32), pltpu.VMEM((2, d, tn), F32),
                            pltpu.VMEM((d, tn), BF16), pltpu.VMEM((d, tn), BF16),
                            pltpu.SemaphoreType.DMA((2,))],
        ),
        out_shape=jax.ShapeDtypeStruct((n_pad, ff), BF16),
        compiler_params=_params("arbitrary", "arbitrary"),
    )(*tables, xs, w_gate, w_up, b_gate.reshape(ne, 1, ff), b_up.reshape(ne, 1, ff))


def _moe_down(hs, tables, w_down, b_down, tm):
    n_pad, ff = hs.shape
    ne, _, d = w_down.shape
    n = d // PACKED_SLAB
    return pl.pallas_call(
        _moe_down_kernel,
        grid_spec=pltpu.PrefetchScalarGridSpec(
            num_scalar_prefetch=len(tables),
            grid=(1, n_pad // tm),
            in_specs=[pl.BlockSpec((tm, ff), lambda j, i, *_: (i, 0)),
                      pl.BlockSpec(memory_space=pl.ANY),
                      pl.BlockSpec((1, 1, d), lambda j, i, be, *_: (be[i], 0, 0))],
            out_specs=pl.BlockSpec((tm * n, LANES), lambda j, i, *_: (i, 0)),
            scratch_shapes=[pltpu.VMEM((2, ff, d), F32), pltpu.VMEM((ff, d), BF16),
                            pltpu.SemaphoreType.DMA((2,))],
        ),
        out_shape=jax.ShapeDtypeStruct((n_pad * n, LANES), jnp.uint32),
        compiler_params=_params("arbitrary", "arbitrary"),
    )(*tables, hs, w_down, b_down.reshape(ne, 1, d))


def _combine_ple_kernel(pos_ref, pos_next_ref, y_ref, w_ref, h_ref, p_ref, gffn_ref, gple_ref,
                        wgate_ref, wproj_ref, gpost_ref, o_ref, buf_ref, ff_ref, sem):
    tm, d = h_ref.shape
    n = d // PACKED_SLAB
    half = d // 2
    i = pl.program_id(0)

    def issue(idx_ref, slot):
        _issue_row_copies(TOP_K * tm, lambda r: idx_ref[0, 0, r], y_ref, buf_ref.at[slot],
                          sem.at[slot], n)

    for slot in range(2):
        @pl.when(i % 2 == slot)
        def _(slot=slot):
            if slot == 0:
                @pl.when(i == 0)
                def _():
                    issue(pos_ref, 0)

            @pl.when(i + 1 < pl.num_programs(0))
            def _():
                issue(pos_next_ref, 1 - slot)

            pltpu.make_async_copy(y_ref.at[pl.ds(0, TOP_K * tm * n)], buf_ref.at[slot],
                                  sem.at[slot]).wait()
            w = w_ref[...]
            first = second = None
            for kk in range(TOP_K):
                a, b = _unpack_bf16_pairs(_load_slabs(buf_ref.at[slot], kk * tm * n, tm, n))
                wk = w[:, kk:kk + 1]
                first = wk * a if first is None else first + wk * a
                second = wk * b if second is None else second + wk * b
            ff_ref[:, :half] = first
            ff_ref[:, half:] = second

    h = h_ref[...] + _rms(ff_ref[...], gffn_ref[...])
    gate = _sigmoid(_dot(_rms(h, gple_ref[...]).astype(BF16), wgate_ref[...]))
    emb = _dot(p_ref[...].astype(BF16), wproj_ref[...])
    o_ref[...] = h + _rms(gate * emb, gpost_ref[...])


def _combine_ple(y_rows, pos, top_w, h, p, ffn_gain, ple_gain, w_ple_gate, w_ple_proj,
                 ple_post_gain, tm):
    t, d = h.shape
    rows = lambda width: pl.BlockSpec((tm, width), lambda i: (i, 0))
    whole = lambda a: pl.BlockSpec(a.shape, lambda i: (0, 0), pipeline_mode=pl.Buffered(1))
    consts = (ffn_gain.reshape(1, d), ple_gain.reshape(1, d), w_ple_gate, w_ple_proj,
              ple_post_gain.reshape(1, d))
    steps = t // tm
    return pl.pallas_call(
        _combine_ple_kernel,
        grid=(steps,),
        in_specs=[
            pl.BlockSpec((1, 1, TOP_K * tm), lambda i: (i, 0, 0), memory_space=pltpu.SMEM),
            pl.BlockSpec((1, 1, TOP_K * tm), lambda i: (jnp.minimum(i + 1, steps - 1), 0, 0),
                         memory_space=pltpu.SMEM),
            pl.BlockSpec(memory_space=pl.ANY),
            rows(TOP_K), rows(d), rows(p.shape[1]),
        ] + [whole(a) for a in consts],
        out_specs=rows(d),
        out_shape=jax.ShapeDtypeStruct((t, d), F32),
        scratch_shapes=[pltpu.VMEM((2, TOP_K * tm * (d // PACKED_SLAB), LANES), jnp.uint32),
                        pltpu.VMEM((tm, d), F32), pltpu.SemaphoreType.DMA((2,))],
        compiler_params=_params("arbitrary"),
    )(pos, pos, y_rows, top_w, h, p, *consts)


def _routing_tables(top_e, tm, tok_block):
    t = top_e.shape[0]
    n = t * TOP_K
    n_blocks = n // tm + N_EXPERTS
    e_flat = top_e.reshape(n)
    experts = jnp.arange(N_EXPERTS, dtype=jnp.int32)
    order = jnp.argsort(e_flat, stable=True).astype(jnp.int32)
    rank = jnp.argsort(order).astype(jnp.int32)
    onehot = e_flat[:, None] == experts[None, :]
    counts = jnp.sum(onehot, axis=0, dtype=jnp.int32)
    group_start = jnp.cumsum(counts) - counts
    padded = ((counts + tm - 1) // tm) * tm
    padded_end = jnp.cumsum(padded)
    padded_start = padded_end - padded
    shift = padded_start - group_start
    pos = (rank + jnp.sum(jnp.where(onehot, shift[None, :], 0), axis=1)).reshape(t, TOP_K)
    n_used = (padded_end[-1] // tm).astype(jnp.int32).reshape(1)
    offs = jnp.arange(n_blocks, dtype=jnp.int32) * tm
    block_e = jnp.sum(padded_end[None, :] <= offs[:, None], axis=1, dtype=jnp.int32)
    last_e = jnp.sum(padded_end <= (n_used[0] - 1) * tm, dtype=jnp.int32)
    block_e = jnp.minimum(jnp.where(offs < padded_end[-1], block_e, last_e), N_EXPERTS - 1)
    sel = block_e[:, None] == experts[None, :]
    pick = lambda table: jnp.sum(jnp.where(sel, table[None, :], 0), axis=1)
    in_group = offs - pick(padded_start)
    block_first = (pick(group_start) + in_group).astype(jnp.int32)
    block_valid = jnp.where(offs < padded_end[-1], jnp.clip(pick(counts) - in_group, 0, tm), 0)
    present = counts > 0
    run_of_expert = jnp.cumsum(present, dtype=jnp.int32) - 1
    n_runs = jnp.sum(present, dtype=jnp.int32).reshape(1)
    block_run = pick(run_of_expert).astype(jnp.int32)
    runs = jnp.arange(N_EXPERTS + 1, dtype=jnp.int32)
    run_expert = jnp.sum(jnp.where(present[None, :] & (run_of_expert[None, :] == runs[:, None]),
                                   experts[None, :], 0), axis=1, dtype=jnp.int32)
    pos_blocks = pos.reshape(t // tok_block, tok_block, TOP_K).transpose(0, 2, 1)
    moe_tables = (block_e, block_valid.astype(jnp.int32), block_run, run_expert, n_runs)
    return (order // TOP_K, block_first, pos_blocks.reshape(t // tok_block, 1, TOP_K * tok_block),
            n_used, moe_tables)


def _layer(h, p, pre_mix_norm, w_in, b_fgate, conv_w, conv_b, dt_bias, a_log, d_skip,
           attn_out_norm, ssm_out_norm, w_out, post_mix_norm, pre_ffn_norm, w_router, b_router,
           w_gate, b_gate, w_up, b_up, w_down, b_down, post_ffn_norm, ple_norm, w_ple_gate,
           w_ple_proj, ple_post_norm):
    bsz, seq, d = h.shape
    t = bsz * seq
    att_w = ATT_HEADS * ATT_HEAD_DIM
    heads = dt_bias.shape[0]
    ssm_w = heads * SSM_HEAD_DIM
    conv_ch = conv_w.shape[1]
    o_fg = 3 * att_w
    o_z = o_fg + ATT_HEADS
    o_xbc = o_z + ssm_w
    o_dt = o_xbc + conv_ch
    x2 = h.reshape(t, d)

    q_scale = LOG2E * ATT_HEAD_DIM ** -0.5
    w_qkv = jnp.concatenate([w_in[:, :att_w] * q_scale, w_in[:, att_w:o_fg]], axis=1).astype(BF16)
    w_z = w_in[:, o_z:o_xbc].astype(BF16)
    w_xbc = w_in[:, o_xbc:o_dt].astype(BF16)
    small_pad = LANES - ATT_HEADS - heads
    w_small = jnp.concatenate([w_in[:, o_fg:o_z], w_in[:, o_dt:], jnp.zeros((d, small_pad), F32)], axis=1)

    tm = min(1024, t)
    qkv = _rms_matmul(x2, pre_mix_norm, w_qkv, BF16, tm, 1024)
    zz = _rms_matmul(x2, pre_mix_norm, w_z, F32, tm, 1024)
    xbc = _rms_matmul(x2, pre_mix_norm, w_xbc, F32, tm, 1024)
    small = _rms_matmul(x2, pre_mix_norm, _split_weight(w_small), F32, tm, LANES, split=True)
    small = small.reshape(bsz, seq, LANES)

    fg_t = jnp.swapaxes(small[:, :, :ATT_HEADS], 1, 2)
    dt_raw = small[:, :, ATT_HEADS:ATT_HEADS + heads]
    c_t = _fg_cumsum(fg_t, b_fgate)
    att = _fox_attention(qkv.reshape(bsz, seq, 3 * att_w), c_t, min(512, seq))
    y = _ssd(xbc.reshape(bsz, seq, conv_ch), zz.reshape(bsz, seq, ssm_w), dt_raw,
             jnp.swapaxes(dt_raw, 1, 2), conv_w, conv_b, dt_bias, a_log, d_skip, ssm_out_norm)

    w_out_bf = w_out.astype(BF16)
    h1, hn, top_e, top_w = _out_proj(att.reshape(t, att_w), y.reshape(t, ssm_w), x2, attn_out_norm,
                                     w_out_bf[:att_w], w_out_bf[att_w:], post_mix_norm,
                                     pre_ffn_norm, w_router, b_router, min(512, t))

    tok_block = min(256, t)
    tok_sorted, block_first, pos, n_used, moe_tables = _routing_tables(top_e, MOE_ROWS, tok_block)
    xs = _gather_rows(hn, tok_sorted, block_first, moe_tables[1], n_used, MOE_ROWS, d, BF16)
    hs = _moe_up(xs, moe_tables, w_gate, b_gate, w_up, b_up, MOE_ROWS, min(1024, w_gate.shape[2]))
    y_rows = _moe_down(hs, moe_tables, w_down, b_down, MOE_ROWS)
    out = _combine_ple(y_rows, pos, top_w, h1, p.reshape(t, -1), post_ffn_norm, ple_norm,
                       w_ple_gate.astype(BF16), w_ple_proj.astype(BF16), ple_post_norm, tok_block)
    return out.reshape(bsz, seq, d)


def kernel(x, p, pre_mix_norm, w_in, b_fgate, conv_w, conv_b, dt_bias, a_log, d_skip, attn_out_norm, ssm_out_norm, w_out, post_mix_norm, pre_ffn_norm, w_router, b_router, w_gate, b_gate, w_up, b_up, w_down, b_down, post_ffn_norm, ple_norm, w_ple_gate, w_ple_proj, ple_post_norm):
    h = x
    for i in range(p.shape[0]):
        h = _layer(h, p[i], pre_mix_norm[i], w_in[i], b_fgate[i], conv_w[i], conv_b[i], dt_bias[i],
                   a_log[i], d_skip[i], attn_out_norm[i], ssm_out_norm[i], w_out[i],
                   post_mix_norm[i], pre_ffn_norm[i], w_router[i], b_router[i], w_gate[i], b_gate[i],
                   w_up[i], b_up[i], w_down[i], b_down[i], post_ffn_norm[i], ple_norm[i],
                   w_ple_gate[i], w_ple_proj[i], ple_post_norm[i])
    return h
```
